```python
import jax, jax.numpy as jnp
from jax import lax
import numpy as np

D_MODEL = 1024
BATCH = 16
SEQ = 2048
DEPTH = 1

CHUNK = 64
Q_BLOCK = 128
HEAD_DIM = 64
N_HEADS_SB = 8
N_HEADS_FOX = 8
D_SB = N_HEADS_SB * HEAD_DIM
D_FOX = N_HEADS_FOX * HEAD_DIM
N_BRANCH = 2
IN_SIZES = (D_SB, D_SB, D_SB, D_SB,
            D_FOX, D_FOX, D_FOX, D_FOX,
            N_HEADS_FOX,
            N_BRANCH * D_MODEL)
D_IN = sum(IN_SIZES)
EPS = 1e-6
NEG_BIG = -1e30

kernel_name = "hybrid_stickbreaking_fox_gated_block"


def _rmsnorm(x, g):
    xf = x.astype(jnp.float32)
    inv = lax.rsqrt(jnp.mean(xf * xf, axis=-1, keepdims=True) + EPS)
    return (xf * inv).astype(x.dtype) * g


def _to_heads(t, n_heads):
    b, s, _ = t.shape
    return t.reshape(b, s, n_heads, HEAD_DIM).transpose(0, 2, 1, 3)


def _from_heads(t):
    b, h, s, d = t.shape
    return t.transpose(0, 2, 1, 3).reshape(b, s, h * d)


def _stick_breaking(q, k, v):
    s_len = q.shape[2]
    scale = HEAD_DIM ** -0.5
    outs = []
    for start in range(0, s_len, Q_BLOCK):
        end = start + Q_BLOCK
        qb, kb, vb = q[:, :, start:end], k[:, :, :end], v[:, :, :end]
        z = jnp.einsum('bhtd,bhsd->bhts', qb, kb).astype(jnp.float32) * scale
        t_idx = start + jnp.arange(Q_BLOCK)[:, None]
        s_idx = jnp.arange(end)[None, :]
        mask = s_idx < t_idx
        log_1m = jnp.where(mask, jax.nn.log_sigmoid(-z), 0.0)
        after = lax.cumsum(log_1m, axis=3, reverse=True) - log_1m
        w = jnp.where(mask, jnp.exp(jax.nn.log_sigmoid(z) + after), 0.0)
        outs.append(jnp.einsum('bhts,bhsd->bhtd', w.astype(vb.dtype), vb))
    return jnp.concatenate(outs, axis=2)


def _forgetting_attention(q, k, v, cum_log_f):
    s_len = q.shape[2]
    scale = HEAD_DIM ** -0.5
    outs = []
    for start in range(0, s_len, Q_BLOCK):
        end = start + Q_BLOCK
        qb, kb, vb = q[:, :, start:end], k[:, :, :end], v[:, :, :end]
        z = jnp.einsum('bhtd,bhsd->bhts', qb, kb).astype(jnp.float32) * scale
        z = z + cum_log_f[:, :, start:end, None] - cum_log_f[:, :, None, :end]
        t_idx = start + jnp.arange(Q_BLOCK)[:, None]
        s_idx = jnp.arange(end)[None, :]
        z = jnp.where(s_idx <= t_idx, z, NEG_BIG)
        p = jax.nn.softmax(z, axis=-1)
        outs.append(jnp.einsum('bhts,bhsd->bhtd', p.astype(vb.dtype), vb))
    return jnp.concatenate(outs, axis=2)


def setup_inputs(seed: int = 0) -> dict:
    key = jax.random.key(seed)
    ks = jax.random.split(key, 12)
    f32 = jnp.float32
    d = D_MODEL
    x = jax.random.normal(ks[0], (BATCH, SEQ, d), f32)
    c = jax.random.normal(ks[1], (BATCH, d), f32)
    w_ada = jax.random.normal(ks[2], (DEPTH, d, 3 * d), f32) * (0.5 * d ** -0.5)
    b_ada = 0.02 * jax.random.normal(ks[3], (DEPTH, 3 * d), f32)
    norm_g = 1.0 + 0.02 * jax.random.normal(ks[4], (DEPTH, d), f32)
    w_in = jax.random.normal(ks[5], (DEPTH, d, D_IN), f32) * d ** -0.5
    b_forget = jax.random.uniform(ks[6], (DEPTH, N_HEADS_FOX), f32, 1.0, 4.0)
    w_o_sb = jax.random.normal(ks[7], (DEPTH, D_SB, d), f32) * D_SB ** -0.5
    w_o_fox = jax.random.normal(ks[8], (DEPTH, D_FOX, d), f32) * D_FOX ** -0.5
    b_gate = 0.02 * jax.random.normal(ks[9], (DEPTH, N_BRANCH * d), f32)
    w_out = jax.random.normal(ks[10], (DEPTH, d, d), f32) * d ** -0.5
    final_g = 1.0 + 0.02 * jax.random.normal(ks[11], (d,), f32)
    return {"x": x, "c": c, "w_ada": w_ada, "b_ada": b_ada, "norm_g": norm_g,
            "w_in": w_in, "b_forget": b_forget, "w_o_sb": w_o_sb, "w_o_fox": w_o_fox,
            "b_gate": b_gate, "w_out": w_out, "final_g": final_g}


def reference(x, c, w_ada, b_ada, norm_g, w_in, b_forget, w_o_sb, w_o_fox, b_gate, w_out, final_g):
    b, s, d = x.shape
    split_points = list(np.cumsum(IN_SIZES)[:-1])
    for l in range(DEPTH):
        mod = c @ w_ada[l] + b_ada[l]
        shift, scale, gate = jnp.split(mod, 3, axis=-1)
        h = _rmsnorm(x, norm_g[l]) * (1.0 + scale[:, None, :]) + shift[:, None, :]

        proj = h @ w_in[l]
        (q_a, k_a, v_a, z_a, q_b, k_b, v_b, z_b,
         f_logit, g_logit) = jnp.split(proj, split_points, axis=-1)

        y_a = _from_heads(_stick_breaking(_to_heads(q_a, N_HEADS_SB), _to_heads(k_a, N_HEADS_SB),
                                          _to_heads(v_a, N_HEADS_SB)))
        y_a = (y_a * jax.nn.silu(z_a)) @ w_o_sb[l]

        log_f = jax.nn.log_sigmoid((f_logit + b_forget[l]).astype(jnp.float32))
        cum_log_f = jnp.cumsum(log_f, axis=1).transpose(0, 2, 1)
        y_b = _from_heads(_forgetting_attention(_to_heads(q_b, N_HEADS_FOX), _to_heads(k_b, N_HEADS_FOX),
                                                _to_heads(v_b, N_HEADS_FOX), cum_log_f))
        y_b = (y_b * jax.nn.silu(z_b)) @ w_o_fox[l]

        gates = jax.nn.sigmoid(g_logit + b_gate[l]).reshape(b, s, N_BRANCH, d)
        merged = gates[:, :, 0, :] * y_a + gates[:, :, 1, :] * y_b
        x = x + gate[:, None, :] * (merged @ w_out[l])
    return _rmsnorm(x, final_g)
```

```python
import functools

import jax
import jax.numpy as jnp
from jax import lax
from jax.experimental import pallas as pl
from jax.experimental.pallas import tpu as pltpu

F32 = jnp.float32
BF16 = jnp.bfloat16

HEAD_DIM = 64
N_HEADS = 8
D_BRANCH = N_HEADS * HEAD_DIM
N_BRANCH = 2
EPS = 1e-6
NEG_BIG = -1e30

LANES = 128
HEADS_PER_STEP = LANES // HEAD_DIM
FEAT_LANES = LANES // N_HEADS
ATTN_BLOCK = 256
HALF = ATTN_BLOCK // 2

VMEM_LIMIT = 48 * 1024 * 1024


def _dot(a, b):
    return jnp.dot(a, b, preferred_element_type=F32)


def _dot_nt(a, b):
    return lax.dot_general(a, b, (((1,), (1,)), ((), ())), preferred_element_type=F32)


def _split2(x):
    hi = x.astype(BF16)
    lo = (x - hi.astype(F32)).astype(BF16)
    return hi, lo


def _split3(x):
    hi = x.astype(BF16)
    r1 = x - hi.astype(F32)
    mid = r1.astype(BF16)
    lo = (r1 - mid.astype(F32)).astype(BF16)
    return hi, mid, lo


def _log_sigmoid(x):
    return jnp.minimum(x, 0.0) - jnp.log(1.0 + jnp.exp(-jnp.abs(x)))


def _mod_kernel(c_ref, w_ref, b_ref, o_ref):
    c_hi, c_lo = _split2(c_ref[...])
    w_hi, w_lo = _split2(w_ref[...])
    acc = _dot(c_hi, w_hi) + _dot(c_hi, w_lo) + _dot(c_lo, w_hi)
    o_ref[0] = acc + b_ref[...]


def _mod(c, w_ada, b_ada):
    b, d = c.shape
    return pl.pallas_call(
        _mod_kernel,
        out_shape=jax.ShapeDtypeStruct((3, b, d), F32),
        grid=(3,),
        in_specs=[
            pl.BlockSpec((b, d), lambda j: (0, 0)),
            pl.BlockSpec((d, d), lambda j: (0, j)),
            pl.BlockSpec((1, d), lambda j: (0, j)),
        ],
        out_specs=pl.BlockSpec((1, b, d), lambda j: (j, 0, 0)),
        compiler_params=pltpu.CompilerParams(
            dimension_semantics=("arbitrary",), vmem_limit_bytes=VMEM_LIMIT),
        name="mod",
    )(c, w_ada, b_ada.reshape(1, 3 * d))


def _proj_kernel(x_ref, mod_ref, g_ref, w_ref, wf_ref, o_ref, f_ref, h_ref):
    @pl.when(pl.program_id(1) == 0)
    def _():
        x = x_ref[...]
        inv = lax.rsqrt(jnp.mean(x * x, axis=-1, keepdims=True) + EPS)
        shift = mod_ref[0, 0:1, :]
        scale = mod_ref[0, 1:2, :]
        h = ((x * inv) * g_ref[...]) * (1.0 + scale) + shift
        hb = h.astype(BF16)
        h_ref[...] = hb
        f_ref[...] = _dot(hb, wf_ref[...])

    o_ref[...] = _dot(h_ref[...], w_ref[...]).astype(BF16)


def _proj(x2, mod3, norm_g, w_main, w_f, seq, tm, tn):
    m, d = x2.shape
    n = w_main.shape[1]
    blocks_per_seq = seq // tm
    return pl.pallas_call(
        _proj_kernel,
        out_shape=(jax.ShapeDtypeStruct((m, n), BF16),
                   jax.ShapeDtypeStruct((m, LANES), F32)),
        grid=(m // tm, n // tn),
        in_specs=[
            pl.BlockSpec((tm, d), lambda i, j: (i, 0)),
            pl.BlockSpec((1, 3, d), lambda i, j: (i // blocks_per_seq, 0, 0)),
            pl.BlockSpec((1, d), lambda i, j: (0, 0)),
            pl.BlockSpec((d, tn), lambda i, j: (0, j)),
            pl.BlockSpec((d, LANES), lambda i, j: (0, 0)),
        ],
        out_specs=(pl.BlockSpec((tm, tn), lambda i, j: (i, j)),
                   pl.BlockSpec((tm, LANES), lambda i, j: (i, 0))),
        scratch_shapes=[pltpu.VMEM((tm, d), BF16)],
        compiler_params=pltpu.CompilerParams(
            dimension_semantics=("arbitrary", "arbitrary"), vmem_limit_bytes=VMEM_LIMIT),
        name="proj",
    )(x2, mod3, norm_g, w_main, w_f)


def _fprep_kernel(f_ref, bf_ref, qf_ref, kf_ref, *, seq):
    blk = ATTN_BLOCK
    row = lax.broadcasted_iota(jnp.int32, (blk, blk), 0)
    col = lax.broadcasted_iota(jnp.int32, (blk, blk), 1)
    tri = jnp.where(col <= row, 1.0, 0.0).astype(BF16)
    r = lax.broadcasted_iota(jnp.int32, (blk, LANES), 1) % FEAT_LANES
    carry = jnp.zeros((1, LANES), F32)
    for i in range(seq // blk):
        rows = pl.ds(i * blk, blk)
        lf = _log_sigmoid(f_ref[0, rows, :] + bf_ref[...])
        hi, mid, lo = _split3(lf)
        cum = (_dot(tri, hi) + _dot(tri, mid)) + _dot(tri, lo) + carry
        carry = cum[blk - 1:blk, :]
        c_hi, c_mid, c_lo = (p.astype(F32) for p in _split3(cum))
        qf = jnp.where(r == 0, c_hi, jnp.where(r == 1, c_mid, jnp.where(r == 2, c_lo,
             jnp.where(r < 6, 1.0, 0.0))))
        kf = jnp.where(r < 3, 1.0, jnp.where(r == 3, -c_hi, jnp.where(r == 4, -c_mid,
             jnp.where(r == 5, -c_lo, 0.0))))
        qf_ref[0, rows, :] = qf.astype(BF16)
        kf_ref[0, rows, :] = kf.astype(BF16)


def _fprep(f3, bf_spread):
    b, s, _ = f3.shape
    return pl.pallas_call(
        functools.partial(_fprep_kernel, seq=s),
        out_shape=(jax.ShapeDtypeStruct((b, s, LANES), BF16),
                   jax.ShapeDtypeStruct((b, s, LANES), BF16)),
        grid=(b,),
        in_specs=[pl.BlockSpec((1, s, LANES), lambda i: (i, 0, 0)),
                  pl.BlockSpec((1, LANES), lambda i: (0, 0))],
        out_specs=(pl.BlockSpec((1, s, LANES), lambda i: (i, 0, 0)),
                   pl.BlockSpec((1, s, LANES), lambda i: (i, 0, 0))),
        compiler_params=pltpu.CompilerParams(
            dimension_semantics=("arbitrary",), vmem_limit_bytes=VMEM_LIMIT),
        name="fprep",
    )(f3, bf_spread)


def _silu(z):
    return z * jax.nn.sigmoid(z)


def _head_mask(shape, head_in_step):
    lane = lax.broadcasted_iota(jnp.int32, shape, len(shape) - 1)
    return (lane // HEAD_DIM) == head_in_step


def _sb_kernel(q_ref, k_ref, v_ref, z_ref, o_ref, kx_ref, t_ref):
    blk = ATTN_BLOCK
    qi = pl.program_id(2)

    @pl.when(qi == 0)
    def _init():
        k = k_ref[0].astype(F32)
        for h in range(HEADS_PER_STEP):
            kx_ref[h] = jnp.where(_head_mask(k.shape, h), k, 0.0).astype(BF16)
        j = lax.broadcasted_iota(jnp.int32, (blk, blk), 0) % HALF
        c = lax.broadcasted_iota(jnp.int32, (blk, blk), 1)
        t_ref[...] = jnp.where((c >= HALF) | (j > c), 1.0, 0.0).astype(BF16)

    q = q_ref[0]
    t_ext = t_ref[...]
    row = lax.broadcasted_iota(jnp.int32, (blk, blk), 0)
    col = lax.broadcasted_iota(jnp.int32, (blk, blk), 1)
    strictly_lower = col < row

    def tile(h, j, carry, acc, diag):
        keys = pl.ds(pl.multiple_of(j * blk, blk), blk)
        z = _dot_nt(q, kx_ref[h, keys, :])
        sp = jnp.log(1.0 + jnp.exp(-jnp.abs(z)))
        l1m = jnp.minimum(-z, 0.0) - sp
        ls = z + l1m
        if diag:
            l1m = jnp.where(strictly_lower, l1m, 0.0)
        res = []
        for half in (1, 0):
            hi, lo = _split2(l1m[:, half * HALF:(half + 1) * HALF])
            res.append(_dot(jnp.concatenate([hi, lo], axis=1), t_ext))
        after_r, sum_r = res[0][:, :HALF], res[0][:, HALF:]
        after_l, sum_l = res[1][:, :HALF], res[1][:, HALF:]
        carry_l = carry + sum_r
        w_r = jnp.exp(ls[:, HALF:] + (after_r + carry))
        w_l = jnp.exp(ls[:, :HALF] + (after_l + carry_l))
        w = jnp.concatenate([w_l, w_r], axis=1)
        if diag:
            w = jnp.where(strictly_lower, w, 0.0)
        acc = acc + _dot(w.astype(BF16), v_ref[0, keys, :])
        return carry_l + sum_l, acc

    outs = []
    for h in range(HEADS_PER_STEP):
        zeros = jnp.zeros((blk, LANES), F32)
        carry, acc = tile(h, qi, zeros, zeros, True)

        def body(i, state, h=h):
            return tile(h, qi - 1 - i, state[0], state[1], False)

        carry, acc = lax.fori_loop(0, qi, body, (carry, acc))
        outs.append(acc)
    y = jnp.where(_head_mask(outs[0].shape, 0), outs[0], outs[1])
    o_ref[0] = (y * _silu(z_ref[0].astype(F32))).astype(BF16)


def _sb_attn(proj3, col0):
    b, s, _ = proj3.shape
    blk = ATTN_BLOCK
    nb = D_BRANCH // LANES
    return pl.pallas_call(
        _sb_kernel,
        out_shape=jax.ShapeDtypeStruct((b, s, D_BRANCH), BF16),
        grid=(b, nb, s // blk),
        in_specs=[
            pl.BlockSpec((1, blk, LANES), lambda bi, hp, qi: (bi, qi, col0 + hp)),
            pl.BlockSpec((1, s, LANES), lambda bi, hp, qi: (bi, 0, col0 + nb + hp)),
            pl.BlockSpec((1, s, LANES), lambda bi, hp, qi: (bi, 0, col0 + 2 * nb + hp)),
            pl.BlockSpec((1, blk, LANES), lambda bi, hp, qi: (bi, qi, col0 + 3 * nb + hp)),
        ],
        out_specs=pl.BlockSpec((1, blk, LANES), lambda bi, hp, qi: (bi, qi, hp)),
        scratch_shapes=[pltpu.VMEM((HEADS_PER_STEP, s, LANES), BF16),
                        pltpu.VMEM((blk, blk), BF16)],
        compiler_params=pltpu.CompilerParams(
            dimension_semantics=("arbitrary", "arbitrary", "arbitrary"),
            vmem_limit_bytes=VMEM_LIMIT),
        name="sb_attn",
    )(proj3, proj3, proj3, proj3)


def _fox_kernel(q_ref, k_ref, v_ref, z_ref, qf_ref, kf_ref, o_ref, kx_ref, vx_ref):
    blk = ATTN_BLOCK
    hp = pl.program_id(1)
    qi = pl.program_id(2)

    @pl.when(qi == 0)
    def _init():
        k = k_ref[0].astype(F32)
        kf = kf_ref[0].astype(F32)
        feat_lane = lax.broadcasted_iota(jnp.int32, kf.shape, 1) // FEAT_LANES
        for h in range(HEADS_PER_STEP):
            km = jnp.where(_head_mask(k.shape, h), k, 0.0)
            kfm = jnp.where(feat_lane == hp * HEADS_PER_STEP + h, kf, 0.0)
            kx_ref[h] = jnp.concatenate([km, kfm], axis=1).astype(BF16)
        v = v_ref[0]
        vx_ref[...] = jnp.concatenate([v, jnp.ones(v.shape, BF16)], axis=1)

    qx = jnp.concatenate([q_ref[0], qf_ref[0]], axis=1)
    row = lax.broadcasted_iota(jnp.int32, (blk, blk), 0)
    col = lax.broadcasted_iota(jnp.int32, (blk, blk), 1)
    causal = col <= row

    def logits(h, j):
        keys = pl.ds(pl.multiple_of(j * blk, blk), blk)
        return _dot_nt(qx, kx_ref[h, keys, :]), keys

    outs = []
    for h in range(HEADS_PER_STEP):
        z, keys = logits(h, qi)
        z = jnp.where(causal, z, NEG_BIG)
        m = jnp.max(z, axis=1, keepdims=True)
        p = jnp.exp(z - m)
        acc = _dot(p.astype(BF16), vx_ref[keys, :])

        def body(i, state, h=h):
            m, acc = state
            z, keys = logits(h, qi - 1 - i)
            m_new = jnp.maximum(m, jnp.max(z, axis=1, keepdims=True))
            alpha = jnp.exp(m - m_new)
            p = jnp.exp(z - m_new)
            acc = alpha * acc + _dot(p.astype(BF16), vx_ref[keys, :])
            return m_new, acc

        m, acc = lax.fori_loop(0, qi, body, (m, acc))
        outs.append(acc[:, :LANES] / acc[:, LANES:])
    y = jnp.where(_head_mask(outs[0].shape, 0), outs[0], outs[1])
    o_ref[0] = (y * _silu(z_ref[0].astype(F32))).astype(BF16)


def _fox_attn(proj3, qfeat, kfeat, col0):
    b, s, _ = proj3.shape
    blk = ATTN_BLOCK
    nb = D_BRANCH // LANES
    return pl.pallas_call(
        _fox_kernel,
        out_shape=jax.ShapeDtypeStruct((b, s, D_BRANCH), BF16),
        grid=(b, nb, s // blk),
        in_specs=[
            pl.BlockSpec((1, blk, LANES), lambda bi, hp, qi: (bi, qi, col0 + hp)),
            pl.BlockSpec((1, s, LANES), lambda bi, hp, qi: (bi, 0, col0 + nb + hp)),
            pl.BlockSpec((1, s, LANES), lambda bi, hp, qi: (bi, 0, col0 + 2 * nb + hp)),
            pl.BlockSpec((1, blk, LANES), lambda bi, hp, qi: (bi, qi, col0 + 3 * nb + hp)),
            pl.BlockSpec((1, blk, LANES), lambda bi, hp, qi: (bi, qi, 0)),
            pl.BlockSpec((1, s, LANES), lambda bi, hp, qi: (bi, 0, 0)),
        ],
        out_specs=pl.BlockSpec((1, blk, LANES), lambda bi, hp, qi: (bi, qi, hp)),
        scratch_shapes=[pltpu.VMEM((HEADS_PER_STEP, s, 2 * LANES), BF16),
                        pltpu.VMEM((s, 2 * LANES), BF16)],
        compiler_params=pltpu.CompilerParams(
            dimension_semantics=("arbitrary", "arbitrary", "arbitrary"),
            vmem_limit_bytes=VMEM_LIMIT),
        name="fox_attn",
    )(proj3, proj3, proj3, proj3, qfeat, kfeat)


def _merge_kernel(ya_ref, yb_ref, g_ref, x_ref, mod_ref, bg_ref, wa_ref, wb_ref, wo_ref,
                  fg_ref, o_ref):
    d = x_ref.shape[1]
    y_a = _dot(ya_ref[...], wa_ref[...])
    y_b = _dot(yb_ref[...], wb_ref[...])
    gates = jax.nn.sigmoid(g_ref[...].astype(F32) + bg_ref[...])
    merged = gates[:, :d] * y_a + gates[:, d:] * y_b
    upd = _dot(merged.astype(BF16), wo_ref[...])
    xn = x_ref[...] + mod_ref[0, 2:3, :] * upd
    inv = lax.rsqrt(jnp.mean(xn * xn, axis=-1, keepdims=True) + EPS)
    o_ref[...] = (xn * inv) * fg_ref[...]


def _merge(ya, yb, proj, gcol, x2, mod3, b_gate, wa, wb, wo, final_g, seq, tm):
    m, d = x2.shape
    blocks_per_seq = seq // tm
    return pl.pallas_call(
        _merge_kernel,
        out_shape=jax.ShapeDtypeStruct((m, d), F32),
        grid=(m // tm,),
        in_specs=[
            pl.BlockSpec((tm, D_BRANCH), lambda i: (i, 0)),
            pl.BlockSpec((tm, D_BRANCH), lambda i: (i, 0)),
            pl.BlockSpec((tm, N_BRANCH * d), lambda i: (i, gcol)),
            pl.BlockSpec((tm, d), lambda i: (i, 0)),
            pl.BlockSpec((1, 3, d), lambda i: (i // blocks_per_seq, 0, 0)),
            pl.BlockSpec((1, N_BRANCH * d), lambda i: (0, 0)),
            pl.BlockSpec((D_BRANCH, d), lambda i: (0, 0)),
            pl.BlockSpec((D_BRANCH, d), lambda i: (0, 0)),
            pl.BlockSpec((d, d), lambda i: (0, 0)),
            pl.BlockSpec((1, d), lambda i: (0, 0)),
        ],
        out_specs=pl.BlockSpec((tm, d), lambda i: (i, 0)),
        compiler_params=pltpu.CompilerParams(
            dimension_semantics=("arbitrary",), vmem_limit_bytes=VMEM_LIMIT),
        name="merge",
    )(ya, yb, proj, x2, mod3, b_gate, wa, wb, wo, final_g)


def kernel(x, c, w_ada, b_ada, norm_g, w_in, b_forget, w_o_sb, w_o_fox, b_gate, w_out, final_g):
    b, s, d = x.shape
    n_qkvz = 8 * D_BRANCH
    assert w_ada.shape[0] == 1, "single-layer trunk only"
    assert w_in.shape[2] == n_qkvz + N_HEADS + N_BRANCH * d
    assert s % ATTN_BLOCK == 0 and d % LANES == 0
    qk_scale = HEAD_DIM ** -0.5

    w = w_in[0]
    col_scale = jnp.ones((n_qkvz,), F32)
    col_scale = col_scale.at[0:D_BRANCH].set(qk_scale)
    col_scale = col_scale.at[4 * D_BRANCH:5 * D_BRANCH].set(qk_scale)
    w_main = jnp.concatenate(
        [w[:, :n_qkvz] * col_scale, w[:, n_qkvz + N_HEADS:]], axis=1).astype(BF16)
    w_f = jnp.repeat(w[:, n_qkvz:n_qkvz + N_HEADS], FEAT_LANES, axis=1).astype(BF16)
    bf_spread = jnp.repeat(b_forget[0], FEAT_LANES).reshape(1, LANES)

    x2 = x.reshape(b * s, d)
    mod3 = jnp.transpose(_mod(c, w_ada[0], b_ada[0]), (1, 0, 2))
    proj, f = _proj(x2, mod3, norm_g[0].reshape(1, d), w_main, w_f, s, tm=1024, tn=1024)
    proj3 = proj.reshape(b, s, proj.shape[1])
    qfeat, kfeat = _fprep(f.reshape(b, s, LANES), bf_spread)
    nb = D_BRANCH // LANES
    ya = _sb_attn(proj3, 0)
    yb = _fox_attn(proj3, qfeat, kfeat, 4 * nb)
    out = _merge(ya.reshape(b * s, D_BRANCH), yb.reshape(b * s, D_BRANCH), proj,
                 n_qkvz // (N_BRANCH * d), x2, mod3, b_gate[0].reshape(1, N_BRANCH * d),
                 w_o_sb[0].astype(BF16), w_o_fox[0].astype(BF16), w_out[0].astype(BF16),
                 final_g.reshape(1, d), s, tm=512)
    return out.reshape(b, s, d)
```

```python
import functools

import jax
import jax.numpy as jnp
from jax import lax
from jax.experimental import pallas as pl
from jax.experimental.pallas import tpu as pltpu

F32 = jnp.float32
BF16 = jnp.bfloat16

HEAD_DIM = 64
N_HEADS = 8
D_BRANCH = N_HEADS * HEAD_DIM
N_BRANCH = 2
EPS = 1e-6
NEG_BIG = -1e30
LOG2E = 1.4426950408889634
EXP2_CLAMP = 126.0

LANES = 128
HEADS_PER_STEP = LANES // HEAD_DIM
FEAT_LANES = LANES // N_HEADS
K_BLOCK = 256
Q_BLOCK = 2 * K_BLOCK

VMEM_LIMIT = 48 * 1024 * 1024


def _dot(a, b):
    return jnp.dot(a, b, preferred_element_type=F32)


def _dot_nt(a, b):
    return lax.dot_general(a, b, (((1,), (1,)), ((), ())), preferred_element_type=F32)


def _split2(x):
    hi = x.astype(BF16)
    lo = (x - hi.astype(F32)).astype(BF16)
    return hi, lo


def _split3(x):
    hi = x.astype(BF16)
    r1 = x - hi.astype(F32)
    mid = r1.astype(BF16)
    lo = (r1 - mid.astype(F32)).astype(BF16)
    return hi, mid, lo


def _log_sigmoid(x):
    return jnp.minimum(x, 0.0) - jnp.log(1.0 + jnp.exp(-jnp.abs(x)))


def _mod_kernel(c_ref, w_ref, b_ref, o_ref):
    c_hi, c_lo = _split2(c_ref[...])
    w_hi, w_lo = _split2(w_ref[...])
    acc = _dot(c_hi, w_hi) + _dot(c_hi, w_lo) + _dot(c_lo, w_hi)
    o_ref[0] = acc + b_ref[...]


def _mod(c, w_ada, b_ada):
    b, d = c.shape
    return pl.pallas_call(
        _mod_kernel,
        out_shape=jax.ShapeDtypeStruct((3, b, d), F32),
        grid=(3,),
        in_specs=[
            pl.BlockSpec((b, d), lambda j: (0, 0)),
            pl.BlockSpec((d, d), lambda j: (0, j)),
            pl.BlockSpec((1, d), lambda j: (0, j)),
        ],
        out_specs=pl.BlockSpec((1, b, d), lambda j: (j, 0, 0)),
        compiler_params=pltpu.CompilerParams(
            dimension_semantics=("arbitrary",), vmem_limit_bytes=VMEM_LIMIT),
        name="mod",
    )(c, w_ada, b_ada.reshape(1, 3 * d))


def _proj_kernel(x_ref, mod_ref, g_ref, w_ref, wf_ref, o_ref, f_ref, h_ref):
    @pl.when(pl.program_id(1) == 0)
    def _():
        x = x_ref[...]
        inv = lax.rsqrt(jnp.mean(x * x, axis=-1, keepdims=True) + EPS)
        shift = mod_ref[0, 0:1, :]
        scale = mod_ref[0, 1:2, :]
        h = ((x * inv) * g_ref[...]) * (1.0 + scale) + shift
        hb = h.astype(BF16)
        h_ref[...] = hb
        f_ref[...] = _dot(hb, wf_ref[...])

    o_ref[...] = _dot(h_ref[...], w_ref[...]).astype(BF16)


def _proj(x2, mod3, norm_g, w_main, w_f, seq, tm, tn):
    m, d = x2.shape
    n = w_main.shape[1]
    blocks_per_seq = seq // tm
    return pl.pallas_call(
        _proj_kernel,
        out_shape=(jax.ShapeDtypeStruct((m, n), BF16),
                   jax.ShapeDtypeStruct((m, LANES), F32)),
        grid=(m // tm, n // tn),
        in_specs=[
            pl.BlockSpec((tm, d), lambda i, j: (i, 0)),
            pl.BlockSpec((1, 3, d), lambda i, j: (i // blocks_per_seq, 0, 0)),
            pl.BlockSpec((1, d), lambda i, j: (0, 0)),
            pl.BlockSpec((d, tn), lambda i, j: (0, j)),
            pl.BlockSpec((d, LANES), lambda i, j: (0, 0)),
        ],
        out_specs=(pl.BlockSpec((tm, tn), lambda i, j: (i, j)),
                   pl.BlockSpec((tm, LANES), lambda i, j: (i, 0))),
        scratch_shapes=[pltpu.VMEM((tm, d), BF16)],
        compiler_params=pltpu.CompilerParams(
            dimension_semantics=("arbitrary", "arbitrary"), vmem_limit_bytes=VMEM_LIMIT),
        name="proj",
    )(x2, mod3, norm_g, w_main, w_f)


def _fprep_kernel(f_ref, bf_ref, qf_ref, kf_ref, *, seq):
    blk = K_BLOCK
    row = lax.broadcasted_iota(jnp.int32, (blk, blk), 0)
    col = lax.broadcasted_iota(jnp.int32, (blk, blk), 1)
    tri = jnp.where(col <= row, 1.0, 0.0).astype(BF16)
    r = lax.broadcasted_iota(jnp.int32, (blk, LANES), 1) % FEAT_LANES
    carry = jnp.zeros((1, LANES), F32)
    for i in range(seq // blk):
        rows = pl.ds(i * blk, blk)
        lf = _log_sigmoid(f_ref[0, rows, :] + bf_ref[...])
        hi, mid, lo = _split3(lf)
        cum = (_dot(tri, hi) + _dot(tri, mid)) + _dot(tri, lo) + carry
        carry = cum[blk - 1:blk, :]
        c_hi, c_mid, c_lo = (p.astype(F32) for p in _split3(cum * LOG2E))
        qf = jnp.where(r == 0, c_hi, jnp.where(r == 1, c_mid, jnp.where(r == 2, c_lo,
             jnp.where(r < 6, 1.0, 0.0))))
        kf = jnp.where(r < 3, 1.0, jnp.where(r == 3, -c_hi, jnp.where(r == 4, -c_mid,
             jnp.where(r == 5, -c_lo, jnp.where(r < 9, 1.0, 0.0)))))
        qf_ref[0, rows, :] = qf.astype(BF16)
        kf_ref[0, rows, :] = kf.astype(BF16)


def _fprep(f3, bf_spread):
    b, s, _ = f3.shape
    return pl.pallas_call(
        functools.partial(_fprep_kernel, seq=s),
        out_shape=(jax.ShapeDtypeStruct((b, s, LANES), BF16),
                   jax.ShapeDtypeStruct((b, s, LANES), BF16)),
        grid=(b,),
        in_specs=[pl.BlockSpec((1, s, LANES), lambda i: (i, 0, 0)),
                  pl.BlockSpec((1, LANES), lambda i: (0, 0))],
        out_specs=(pl.BlockSpec((1, s, LANES), lambda i: (i, 0, 0)),
                   pl.BlockSpec((1, s, LANES), lambda i: (i, 0, 0))),
        compiler_params=pltpu.CompilerParams(
            dimension_semantics=("arbitrary",), vmem_limit_bytes=VMEM_LIMIT),
        name="fprep",
    )(f3, bf_spread)


def _silu(z):
    return z * jax.nn.sigmoid(z)


def _head_mask(shape, head_in_step):
    lane = lax.broadcasted_iota(jnp.int32, shape, len(shape) - 1)
    return (lane // HEAD_DIM) == head_in_step


def _sb_kernel(q_ref, k_ref, v_ref, z_ref, o_ref, kx_ref, t_ref):
    tk = K_BLOCK
    tq = q_ref.shape[1]
    nsub = tq // tk
    qi = pl.program_id(2)

    @pl.when(qi == 0)
    def _init():
        k = k_ref[0].astype(F32)
        for h in range(HEADS_PER_STEP):
            kx_ref[h] = jnp.where(_head_mask(k.shape, h), k, 0.0).astype(BF16)
        j = lax.broadcasted_iota(jnp.int32, (tk, tk), 0)
        c = lax.broadcasted_iota(jnp.int32, (tk, tk), 1)
        t_ref[...] = jnp.where(j > c, 1.0, 0.0).astype(BF16)

    q = q_ref[0]
    row = lax.broadcasted_iota(jnp.int32, (tk, tk), 0)
    col = lax.broadcasted_iota(jnp.int32, (tk, tk), 1)
    strictly_lower = col < row

    def tile(q_rows, h, key_start, carry, acc, mask):
        keys = pl.ds(pl.multiple_of(key_start, tk), tk)
        z = _dot_nt(q_rows, kx_ref[h, keys, :])
        lg = jnp.log(1.0 + jnp.exp2(jnp.minimum(z, EXP2_CLAMP))) * LOG2E
        sp = jnp.maximum(lg, z)
        ls = z - sp
        if mask is not None:
            sp = jnp.where(mask, sp, 0.0)
        sp_b = sp.astype(BF16)
        after = _dot(sp_b, t_ref[...]) + carry
        w = jnp.exp2(ls - after)
        if mask is not None:
            w = jnp.where(mask, w, 0.0)
        acc = acc + _dot(w.astype(BF16), v_ref[0, keys, :])
        return [after[:, 0:1] + sp_b[:, 0:1].astype(F32), acc]

    base = qi * tq
    state = []
    for h in range(HEADS_PER_STEP):
        carries, accs = [], []
        for r in range(nsub):
            q_r = q[r * tk:(r + 1) * tk]
            carry, acc = tile(q_r, h, base + r * tk, jnp.zeros((tk, 1), F32),
                              jnp.zeros((tk, LANES), F32), strictly_lower)
            for c in range(r - 1, -1, -1):
                carry, acc = tile(q_r, h, base + c * tk, carry, acc, None)
            carries.append(carry)
            accs.append(acc)
        state += [jnp.concatenate(carries, axis=0), jnp.concatenate(accs, axis=0)]

    def body(i, state):
        state = list(state)
        for c in range(nsub - 1, -1, -1):
            key_start = (qi - 1 - i) * tq + c * tk
            for h in range(HEADS_PER_STEP):
                state[2 * h:2 * h + 2] = tile(q, h, key_start, state[2 * h], state[2 * h + 1], None)
        return tuple(state)

    state = lax.fori_loop(0, qi, body, tuple(state))
    y = jnp.where(_head_mask(state[1].shape, 0), state[1], state[3])
    o_ref[0] = (y * _silu(z_ref[0].astype(F32))).astype(BF16)


def _sb_attn(proj3, col0):
    b, s, _ = proj3.shape
    tq = Q_BLOCK
    nb = D_BRANCH // LANES
    return pl.pallas_call(
        _sb_kernel,
        out_shape=jax.ShapeDtypeStruct((b, s, D_BRANCH), BF16),
        grid=(b, nb, s // tq),
        in_specs=[
            pl.BlockSpec((1, tq, LANES), lambda bi, hp, qi: (bi, qi, col0 + hp)),
            pl.BlockSpec((1, s, LANES), lambda bi, hp, qi: (bi, 0, col0 + nb + hp)),
            pl.BlockSpec((1, s, LANES), lambda bi, hp, qi: (bi, 0, col0 + 2 * nb + hp)),
            pl.BlockSpec((1, tq, LANES), lambda bi, hp, qi: (bi, qi, col0 + 3 * nb + hp)),
        ],
        out_specs=pl.BlockSpec((1, tq, LANES), lambda bi, hp, qi: (bi, qi, hp)),
        scratch_shapes=[pltpu.VMEM((HEADS_PER_STEP, s, LANES), BF16),
                        pltpu.VMEM((K_BLOCK, K_BLOCK), BF16)],
        compiler_params=pltpu.CompilerParams(
            dimension_semantics=("arbitrary", "arbitrary", "arbitrary"),
            vmem_limit_bytes=VMEM_LIMIT),
        name="sb_attn",
    )(proj3, proj3, proj3, proj3)


def _fox_kernel(q_ref, k_ref, v_ref, z_ref, qf_ref, kf_ref, o_ref, kx_ref, vx_ref):
    tk = K_BLOCK
    tq = q_ref.shape[1]
    nsub = tq // tk
    hp = pl.program_id(1)
    qi = pl.program_id(2)

    @pl.when(qi == 0)
    def _init():
        k = k_ref[0].astype(F32)
        kf = kf_ref[0].astype(F32)
        feat_lane = lax.broadcasted_iota(jnp.int32, kf.shape, 1) // FEAT_LANES
        for h in range(HEADS_PER_STEP):
            km = jnp.where(_head_mask(k.shape, h), k, 0.0)
            kfm = jnp.where(feat_lane == hp * HEADS_PER_STEP + h, kf, 0.0)
            kx_ref[h] = jnp.concatenate([km, kfm], axis=1).astype(BF16)
        v = v_ref[0].astype(F32)
        for h in range(HEADS_PER_STEP):
            vx_ref[h] = jnp.where(_head_mask(v.shape, h), v, 1.0).astype(BF16)

    heads = range(HEADS_PER_STEP)
    row = lax.broadcasted_iota(jnp.int32, (tk, tk), 0)
    col = lax.broadcasted_iota(jnp.int32, (tk, tk), 1)
    causal = col <= row
    base = qi * tq

    def logits(qx_rows, h, key_start):
        keys = pl.ds(pl.multiple_of(key_start, tk), tk)
        return _dot_nt(qx_rows, kx_ref[h, keys, :]), keys

    def over_own_span(qx, first, rest):
        out = []
        for h in heads:
            parts = []
            for r in range(nsub):
                qx_r = qx[r * tk:(r + 1) * tk]
                val = first(qx_r, h, base + r * tk)
                for c in range(r - 1, -1, -1):
                    val = rest(val, qx_r, h, base + c * tk)
                parts.append(val)
            out.append(jnp.concatenate(parts, axis=0))
        return tuple(out)

    def halves_max(z):
        return jnp.maximum(z[:, :LANES], z[:, LANES:])

    qx = jnp.concatenate([q_ref[0], qf_ref[0]], axis=1)
    zmax = over_own_span(
        qx,
        lambda qx_r, h, ks: halves_max(jnp.where(causal, logits(qx_r, h, ks)[0], NEG_BIG)),
        lambda val, qx_r, h, ks: jnp.maximum(val, halves_max(logits(qx_r, h, ks)[0])))
    def left_span(step, state):
        def body(i, state):
            state = list(state)
            for c in range(nsub):
                for h in heads:
                    state[h] = step(state[h], h, (qi - 1 - i) * tq + c * tk)
            return tuple(state)
        return lax.fori_loop(0, qi, body, state)

    zmax = left_span(
        lambda val, h, ks: jnp.maximum(val, halves_max(logits(qx, h, ks)[0])), zmax)

    feat = qf_ref[0].astype(F32)
    lane = lax.broadcasted_iota(jnp.int32, feat.shape, 1)
    for h in heads:
        m = jnp.broadcast_to(jnp.max(zmax[h], axis=1, keepdims=True), feat.shape)
        parts = _split3(m)
        for i, part in enumerate(parts):
            here = lane == (hp * HEADS_PER_STEP + h) * FEAT_LANES + 6 + i
            feat = jnp.where(here, -part.astype(F32), feat)
    qx = jnp.concatenate([q_ref[0], feat.astype(BF16)], axis=1)

    def pv(qx_rows, h, key_start, mask):
        z, keys = logits(qx_rows, h, key_start)
        if mask is not None:
            z = jnp.where(mask, z, NEG_BIG)
        return _dot(jnp.exp2(z).astype(BF16), vx_ref[h, keys, :])

    acc = over_own_span(
        qx,
        lambda qx_r, h, ks: pv(qx_r, h, ks, causal),
        lambda val, qx_r, h, ks: val + pv(qx_r, h, ks, None))
    acc = left_span(lambda val, h, ks: val + pv(qx, h, ks, None), acc)
    outs = [a / pltpu.roll(a, HEAD_DIM, axis=1) for a in acc]
    y = jnp.where(_head_mask(outs[0].shape, 0), outs[0], outs[1])
    o_ref[0] = (y * _silu(z_ref[0].astype(F32))).astype(BF16)


def _fox_attn(proj3, qfeat, kfeat, col0):
    b, s, _ = proj3.shape
    tq = Q_BLOCK
    nb = D_BRANCH // LANES
    return pl.pallas_call(
        _fox_kernel,
        out_shape=jax.ShapeDtypeStruct((b, s, D_BRANCH), BF16),
        grid=(b, nb, s // tq),
        in_specs=[
            pl.BlockSpec((1, tq, LANES), lambda bi, hp, qi: (bi, qi, col0 + hp)),
            pl.BlockSpec((1, s, LANES), lambda bi, hp, qi: (bi, 0, col0 + nb + hp)),
            pl.BlockSpec((1, s, LANES), lambda bi, hp, qi: (bi, 0, col0 + 2 * nb + hp)),
            pl.BlockSpec((1, tq, LANES), lambda bi, hp, qi: (bi, qi, col0 + 3 * nb + hp)),
            pl.BlockSpec((1, tq, LANES), lambda bi, hp, qi: (bi, qi, 0)),
            pl.BlockSpec((1, s, LANES), lambda bi, hp, qi: (bi, 0, 0)),
        ],
        out_specs=pl.BlockSpec((1, tq, LANES), lambda bi, hp, qi: (bi, qi, hp)),
        scratch_shapes=[pltpu.VMEM((HEADS_PER_STEP, s, 2 * LANES), BF16),
                        pltpu.VMEM((HEADS_PER_STEP, s, LANES), BF16)],
        compiler_params=pltpu.CompilerParams(
            dimension_semantics=("arbitrary", "arbitrary", "arbitrary"),
            vmem_limit_bytes=VMEM_LIMIT),
        name="fox_attn",
    )(proj3, proj3, proj3, proj3, qfeat, kfeat)


def _merge_kernel(ya_ref, yb_ref, g_ref, x_ref, mod_ref, bg_ref, wa_ref, wb_ref, wo_ref,
                  fg_ref, o_ref):
    d = x_ref.shape[1]
    y_a = _dot(ya_ref[...], wa_ref[...])
    y_b = _dot(yb_ref[...], wb_ref[...])
    gates = jax.nn.sigmoid(g_ref[...].astype(F32) + bg_ref[...])
    merged = gates[:, :d] * y_a + gates[:, d:] * y_b
    upd = _dot(merged.astype(BF16), wo_ref[...])
    xn = x_ref[...] + mod_ref[0, 2:3, :] * upd
    inv = lax.rsqrt(jnp.mean(xn * xn, axis=-1, keepdims=True) + EPS)
    o_ref[...] = (xn * inv) * fg_ref[...]


def _merge(ya, yb, proj, gcol, x2, mod3, b_gate, wa, wb, wo, final_g, seq, tm):
    m, d = x2.shape
    blocks_per_seq = seq // tm
    return pl.pallas_call(
        _merge_kernel,
        out_shape=jax.ShapeDtypeStruct((m, d), F32),
        grid=(m // tm,),
        in_specs=[
            pl.BlockSpec((tm, D_BRANCH), lambda i: (i, 0)),
            pl.BlockSpec((tm, D_BRANCH), lambda i: (i, 0)),
            pl.BlockSpec((tm, N_BRANCH * d), lambda i: (i, gcol)),
            pl.BlockSpec((tm, d), lambda i: (i, 0)),
            pl.BlockSpec((1, 3, d), lambda i: (i // blocks_per_seq, 0, 0)),
            pl.BlockSpec((1, N_BRANCH * d), lambda i: (0, 0)),
            pl.BlockSpec((D_BRANCH, d), lambda i: (0, 0)),
            pl.BlockSpec((D_BRANCH, d), lambda i: (0, 0)),
            pl.BlockSpec((d, d), lambda i: (0, 0)),
            pl.BlockSpec((1, d), lambda i: (0, 0)),
        ],
        out_specs=pl.BlockSpec((tm, d), lambda i: (i, 0)),
        compiler_params=pltpu.CompilerParams(
            dimension_semantics=("arbitrary",), vmem_limit_bytes=VMEM_LIMIT),
        name="merge",
    )(ya, yb, proj, x2, mod3, b_gate, wa, wb, wo, final_g)


def kernel(x, c, w_ada, b_ada, norm_g, w_in, b_forget, w_o_sb, w_o_fox, b_gate, w_out, final_g):
    b, s, d = x.shape
    n_qkvz = 8 * D_BRANCH
    assert w_ada.shape[0] == 1, "single-layer trunk only"
    assert w_in.shape[2] == n_qkvz + N_HEADS + N_BRANCH * d
    assert s % Q_BLOCK == 0 and d % LANES == 0
    qk_scale = HEAD_DIM ** -0.5 * LOG2E

    w = w_in[0]
    col_scale = jnp.ones((n_qkvz,), F32)
    col_scale = col_scale.at[0:D_BRANCH].set(qk_scale)
    col_scale = col_scale.at[4 * D_BRANCH:5 * D_BRANCH].set(qk_scale)
    w_main = jnp.concatenate(
        [w[:, :n_qkvz] * col_scale, w[:, n_qkvz + N_HEADS:]], axis=1).astype(BF16)
    w_f = jnp.repeat(w[:, n_qkvz:n_qkvz + N_HEADS], FEAT_LANES, axis=1).astype(BF16)
    bf_spread = jnp.repeat(b_forget[0], FEAT_LANES).reshape(1, LANES)

    x2 = x.reshape(b * s, d)
    mod3 = jnp.transpose(_mod(c, w_ada[0], b_ada[0]), (1, 0, 2))
    proj, f = _proj(x2, mod3, norm_g[0].reshape(1, d), w_main, w_f, s, tm=1024, tn=1024)
    proj3 = proj.reshape(b, s, proj.shape[1])
    qfeat, kfeat = _fprep(f.reshape(b, s, LANES), bf_spread)
    nb = D_BRANCH // LANES
    ya = _sb_attn(proj3, 0)
    yb = _fox_attn(proj3, qfeat, kfeat, 4 * nb)
    out = _merge(ya.reshape(b * s, D_BRANCH), yb.reshape(b * s, D_BRANCH), proj,
                 n_qkvz // (N_BRANCH * d), x2, mod3, b_gate[0].reshape(1, N_BRANCH * d),
                 w_o_sb[0].astype(BF16), w_o_fox[0].astype(BF16), w_out[0].astype(BF16),
                 final_g.reshape(1, d), s, tm=512)
    return out.reshape(b, s, d)
```

```python
import functools

import jax
import jax.numpy as jnp
from jax import lax
from jax.experimental import pallas as pl
from jax.experimental.pallas import tpu as pltpu

F32 = jnp.float32
BF16 = jnp.bfloat16

HEAD_DIM = 64
N_HEADS = 8
D_BRANCH = N_HEADS * HEAD_DIM
N_BRANCH = 2
EPS = 1e-6
NEG_BIG = -1e30
LOG2E = 1.4426950408889634
EXP2_CLAMP = 126.0

LANES = 128
HEADS_PER_STEP = LANES // HEAD_DIM
FEAT_LANES = LANES // N_HEADS
K_BLOCK = 256
Q_BLOCK = 2 * K_BLOCK

VMEM_LIMIT = 48 * 1024 * 1024


def _dot(a, b):
    return jnp.dot(a, b, preferred_element_type=F32)


def _dot_nt(a, b):
    return lax.dot_general(a, b, (((1,), (1,)), ((), ())), preferred_element_type=F32)


def _split2(x):
    hi = x.astype(BF16)
    lo = (x - hi.astype(F32)).astype(BF16)
    return hi, lo


def _split3(x):
    hi = x.astype(BF16)
    r1 = x - hi.astype(F32)
    mid = r1.astype(BF16)
    lo = (r1 - mid.astype(F32)).astype(BF16)
    return hi, mid, lo


def _log_sigmoid(x):
    return jnp.minimum(x, 0.0) - jnp.log(1.0 + jnp.exp(-jnp.abs(x)))


def _mod_kernel(c_ref, w_ref, b_ref, o_ref):
    c_hi, c_lo = _split2(c_ref[...])
    w_hi, w_lo = _split2(w_ref[...])
    acc = _dot(c_hi, w_hi) + _dot(c_hi, w_lo) + _dot(c_lo, w_hi)
    o_ref[0] = acc + b_ref[...]


def _mod(c, w_ada, b_ada):
    b, d = c.shape
    return pl.pallas_call(
        _mod_kernel,
        out_shape=jax.ShapeDtypeStruct((3, b, d), F32),
        grid=(3,),
        in_specs=[
            pl.BlockSpec((b, d), lambda j: (0, 0)),
            pl.BlockSpec((d, d), lambda j: (0, j)),
            pl.BlockSpec((1, d), lambda j: (0, j)),
        ],
        out_specs=pl.BlockSpec((1, b, d), lambda j: (j, 0, 0)),
        compiler_params=pltpu.CompilerParams(
            dimension_semantics=("arbitrary",), vmem_limit_bytes=VMEM_LIMIT),
        name="mod",
    )(c, w_ada, b_ada.reshape(1, 3 * d))


def _proj_kernel(x_ref, mod_ref, g_ref, w_ref, wf_ref, o_ref, f_ref, h_ref):
    @pl.when(pl.program_id(1) == 0)
    def _():
        x = x_ref[...]
        inv = lax.rsqrt(jnp.mean(x * x, axis=-1, keepdims=True) + EPS)
        shift = mod_ref[0, 0:1, :]
        gain = g_ref[...] * (1.0 + mod_ref[0, 1:2, :])
        h = (x * inv) * gain + shift
        hb = h.astype(BF16)
        h_ref[...] = hb
        f_ref[...] = _dot(hb, wf_ref[...])

    o_ref[...] = _dot(h_ref[...], w_ref[...]).astype(BF16)


def _proj(x2, mod3, norm_g, w_main, w_f, seq, tm, tn):
    m, d = x2.shape
    n = w_main.shape[1]
    blocks_per_seq = seq // tm
    return pl.pallas_call(
        _proj_kernel,
        out_shape=(jax.ShapeDtypeStruct((m, n), BF16),
                   jax.ShapeDtypeStruct((m, LANES), F32)),
        grid=(m // tm, n // tn),
        in_specs=[
            pl.BlockSpec((tm, d), lambda i, j: (i, 0)),
            pl.BlockSpec((1, 3, d), lambda i, j: (i // blocks_per_seq, 0, 0)),
            pl.BlockSpec((1, d), lambda i, j: (0, 0)),
            pl.BlockSpec((d, tn), lambda i, j: (0, j)),
            pl.BlockSpec((d, LANES), lambda i, j: (0, 0)),
        ],
        out_specs=(pl.BlockSpec((tm, tn), lambda i, j: (i, j)),
                   pl.BlockSpec((tm, LANES), lambda i, j: (i, 0))),
        scratch_shapes=[pltpu.VMEM((tm, d), BF16)],
        compiler_params=pltpu.CompilerParams(
            dimension_semantics=("arbitrary", "arbitrary"), vmem_limit_bytes=VMEM_LIMIT),
        name="proj",
    )(x2, mod3, norm_g, w_main, w_f)


def _fprep_kernel(f_ref, bf_ref, qf_ref, kf_ref, *, seq):
    blk = K_BLOCK
    row = lax.broadcasted_iota(jnp.int32, (blk, blk), 0)
    col = lax.broadcasted_iota(jnp.int32, (blk, blk), 1)
    tri = jnp.where(col <= row, 1.0, 0.0).astype(BF16)
    r = lax.broadcasted_iota(jnp.int32, (blk, LANES), 1) % FEAT_LANES
    carry = jnp.zeros((1, LANES), F32)
    for i in range(seq // blk):
        rows = pl.ds(i * blk, blk)
        lf = _log_sigmoid(f_ref[0, rows, :] + bf_ref[...])
        hi, mid, lo = _split3(lf)
        cum = (_dot(tri, hi) + _dot(tri, mid)) + _dot(tri, lo) + carry
        carry = cum[blk - 1:blk, :]
        c_hi, c_mid, c_lo = (p.astype(F32) for p in _split3(cum * LOG2E))
        qf = jnp.where(r == 0, c_hi, jnp.where(r == 1, c_mid, jnp.where(r == 2, c_lo,
             jnp.where(r < 6, 1.0, 0.0))))
        kf = jnp.where(r < 3, 1.0, jnp.where(r == 3, -c_hi, jnp.where(r == 4, -c_mid,
             jnp.where(r == 5, -c_lo, jnp.where(r < 9, 1.0, 0.0)))))
        qf_ref[0, rows, :] = qf.astype(BF16)
        kf_ref[0, rows, :] = kf.astype(BF16)


def _fprep(f3, bf_spread):
    b, s, _ = f3.shape
    return pl.pallas_call(
        functools.partial(_fprep_kernel, seq=s),
        out_shape=(jax.ShapeDtypeStruct((b, s, LANES), BF16),
                   jax.ShapeDtypeStruct((b, s, LANES), BF16)),
        grid=(b,),
        in_specs=[pl.BlockSpec((1, s, LANES), lambda i: (i, 0, 0)),
                  pl.BlockSpec((1, LANES), lambda i: (0, 0))],
        out_specs=(pl.BlockSpec((1, s, LANES), lambda i: (i, 0, 0)),
                   pl.BlockSpec((1, s, LANES), lambda i: (i, 0, 0))),
        compiler_params=pltpu.CompilerParams(
            dimension_semantics=("arbitrary",), vmem_limit_bytes=VMEM_LIMIT),
        name="fprep",
    )(f3, bf_spread)


def _silu(z):
    return z * jax.nn.sigmoid(z)


def _head_mask(shape, head_in_step):
    lane = lax.broadcasted_iota(jnp.int32, shape, len(shape) - 1)
    return (lane // HEAD_DIM) == head_in_step


def _sb_kernel(q_ref, k_ref, v_ref, z_ref, o_ref, kx_ref, t_ref):
    tk = K_BLOCK
    tq = q_ref.shape[1]
    nsub = tq // tk
    qi = pl.program_id(2)

    @pl.when(qi == 0)
    def _init():
        k = k_ref[0].astype(F32)
        for h in range(HEADS_PER_STEP):
            kx_ref[h] = jnp.where(_head_mask(k.shape, h), k, 0.0).astype(BF16)
        j = lax.broadcasted_iota(jnp.int32, (tk, tk), 0)
        c = lax.broadcasted_iota(jnp.int32, (tk, tk), 1)
        t_ref[...] = jnp.where(j > c, 1.0, 0.0).astype(BF16)

    q = q_ref[0]
    row = lax.broadcasted_iota(jnp.int32, (tk, tk), 0)
    col = lax.broadcasted_iota(jnp.int32, (tk, tk), 1)
    strictly_lower = col < row

    def tile(q_rows, h, key_start, carry, acc, mask):
        keys = pl.ds(key_start, tk)
        z = _dot_nt(q_rows, kx_ref[h, keys, :])
        lg = jnp.log(1.0 + jnp.exp2(jnp.minimum(z, EXP2_CLAMP))) * LOG2E
        sp = jnp.maximum(lg, z)
        ls = z - sp
        if mask is not None:
            sp = jnp.where(mask, sp, 0.0)
        sp_b = sp.astype(BF16)
        after = _dot(sp_b, t_ref[...]) + carry
        w = jnp.exp2(ls - after)
        if mask is not None:
            w = jnp.where(mask, w, 0.0)
        acc = acc + _dot(w.astype(BF16), v_ref[0, keys, :])
        return [after[:, 0:1] + sp_b[:, 0:1].astype(F32), acc]

    def sweep(qb):
        base = qb * tq
        state = []
        for h in range(HEADS_PER_STEP):
            carries, accs = [], []
            for r in range(nsub):
                q_r = q[r * tk:(r + 1) * tk]
                carry, acc = tile(q_r, h, base + r * tk, jnp.zeros((tk, 1), F32),
                                  jnp.zeros((tk, LANES), F32), strictly_lower)
                for c in range(r - 1, -1, -1):
                    carry, acc = tile(q_r, h, base + c * tk, carry, acc, None)
                carries.append(carry)
                accs.append(acc)
            state += [jnp.concatenate(carries, axis=0), jnp.concatenate(accs, axis=0)]
        for kb in range(qb * nsub - 1, -1, -1):
            for h in range(HEADS_PER_STEP):
                state[2 * h:2 * h + 2] = tile(q, h, kb * tk, state[2 * h], state[2 * h + 1], None)
        y = jnp.where(_head_mask(state[1].shape, 0), state[1], state[3])
        o_ref[0] = (y * _silu(z_ref[0].astype(F32))).astype(BF16)

    for qb in range(k_ref.shape[1] // tq):
        pl.when(qi == qb)(functools.partial(sweep, qb))


def _sb_attn(proj3, col0):
    b, s, _ = proj3.shape
    tq = Q_BLOCK
    nb = D_BRANCH // LANES
    return pl.pallas_call(
        _sb_kernel,
        out_shape=jax.ShapeDtypeStruct((b, s, D_BRANCH), BF16),
        grid=(b, nb, s // tq),
        in_specs=[
            pl.BlockSpec((1, tq, LANES), lambda bi, hp, qi: (bi, qi, col0 + hp)),
            pl.BlockSpec((1, s, LANES), lambda bi, hp, qi: (bi, 0, col0 + nb + hp)),
            pl.BlockSpec((1, s, LANES), lambda bi, hp, qi: (bi, 0, col0 + 2 * nb + hp)),
            pl.BlockSpec((1, tq, LANES), lambda bi, hp, qi: (bi, qi, col0 + 3 * nb + hp)),
        ],
        out_specs=pl.BlockSpec((1, tq, LANES), lambda bi, hp, qi: (bi, qi, hp)),
        scratch_shapes=[pltpu.VMEM((HEADS_PER_STEP, s, LANES), BF16),
                        pltpu.VMEM((K_BLOCK, K_BLOCK), BF16)],
        compiler_params=pltpu.CompilerParams(
            dimension_semantics=("arbitrary", "arbitrary", "arbitrary"),
            vmem_limit_bytes=VMEM_LIMIT),
        name="sb_attn",
    )(proj3, proj3, proj3, proj3)


def _fox_kernel(q_ref, k_ref, v_ref, z_ref, qf_ref, kf_ref, o_ref, kx_ref, vx_ref):
    tk = K_BLOCK
    tq = q_ref.shape[1]
    nsub = tq // tk
    hp = pl.program_id(1)
    qi = pl.program_id(2)

    @pl.when(qi == 0)
    def _init():
        k = k_ref[0].astype(F32)
        kf = kf_ref[0].astype(F32)
        feat_lane = lax.broadcasted_iota(jnp.int32, kf.shape, 1) // FEAT_LANES
        for h in range(HEADS_PER_STEP):
            km = jnp.where(_head_mask(k.shape, h), k, 0.0)
            kfm = jnp.where(feat_lane == hp * HEADS_PER_STEP + h, kf, 0.0)
            kx_ref[h] = jnp.concatenate([km, kfm], axis=1).astype(BF16)
        v = v_ref[0].astype(F32)
        for h in range(HEADS_PER_STEP):
            vx_ref[h] = jnp.where(_head_mask(v.shape, h), v, 1.0).astype(BF16)

    for qb in range(k_ref.shape[1] // tq):
        pl.when(qi == qb)(functools.partial(
            _fox_sweep, qb, hp, q_ref, z_ref, qf_ref, o_ref, kx_ref, vx_ref))


def _fox_sweep(qb, hp, q_ref, z_ref, qf_ref, o_ref, kx_ref, vx_ref):
    tk = K_BLOCK
    tq = q_ref.shape[1]
    nsub = tq // tk
    heads = range(HEADS_PER_STEP)
    row = lax.broadcasted_iota(jnp.int32, (tk, tk), 0)
    col = lax.broadcasted_iota(jnp.int32, (tk, tk), 1)
    causal = col <= row
    base = qb * tq

    def logits(qx_rows, h, key_start):
        keys = pl.ds(key_start, tk)
        return _dot_nt(qx_rows, kx_ref[h, keys, :]), keys

    def over_own_span(qx, first, rest):
        out = []
        for h in heads:
            parts = []
            for r in range(nsub):
                qx_r = qx[r * tk:(r + 1) * tk]
                val = first(qx_r, h, base + r * tk)
                for c in range(r - 1, -1, -1):
                    val = rest(val, qx_r, h, base + c * tk)
                parts.append(val)
            out.append(jnp.concatenate(parts, axis=0))
        return tuple(out)

    def halves_max(z):
        return jnp.maximum(z[:, :LANES], z[:, LANES:])

    qx = jnp.concatenate([q_ref[0], qf_ref[0]], axis=1)
    zmax = over_own_span(
        qx,
        lambda qx_r, h, ks: halves_max(jnp.where(causal, logits(qx_r, h, ks)[0], NEG_BIG)),
        lambda val, qx_r, h, ks: jnp.maximum(val, halves_max(logits(qx_r, h, ks)[0])))
    def left_span(step, state):
        state = list(state)
        for kb in range(qb * nsub - 1, -1, -1):
            for h in heads:
                state[h] = step(state[h], h, kb * tk)
        return tuple(state)

    zmax = left_span(
        lambda val, h, ks: jnp.maximum(val, halves_max(logits(qx, h, ks)[0])), zmax)

    feat = qf_ref[0].astype(F32)
    lane = lax.broadcasted_iota(jnp.int32, feat.shape, 1)
    for h in heads:
        m = jnp.broadcast_to(jnp.max(zmax[h], axis=1, keepdims=True), feat.shape)
        parts = _split3(m)
        for i, part in enumerate(parts):
            here = lane == (hp * HEADS_PER_STEP + h) * FEAT_LANES + 6 + i
            feat = jnp.where(here, -part.astype(F32), feat)
    qx = jnp.concatenate([q_ref[0], feat.astype(BF16)], axis=1)

    def pv(qx_rows, h, key_start, mask):
        z, keys = logits(qx_rows, h, key_start)
        if mask is not None:
            z = jnp.where(mask, z, NEG_BIG)
        return _dot(jnp.exp2(z).astype(BF16), vx_ref[h, keys, :])

    acc = over_own_span(
        qx,
        lambda qx_r, h, ks: pv(qx_r, h, ks, causal),
        lambda val, qx_r, h, ks: val + pv(qx_r, h, ks, None))
    acc = left_span(lambda val, h, ks: val + pv(qx, h, ks, None), acc)
    outs = [a / pltpu.roll(a, HEAD_DIM, axis=1) for a in acc]
    y = jnp.where(_head_mask(outs[0].shape, 0), outs[0], outs[1])
    o_ref[0] = (y * _silu(z_ref[0].astype(F32))).astype(BF16)


def _fox_attn(proj3, qfeat, kfeat, col0):
    b, s, _ = proj3.shape
    tq = Q_BLOCK
    nb = D_BRANCH // LANES
    return pl.pallas_call(
        _fox_kernel,
        out_shape=jax.ShapeDtypeStruct((b, s, D_BRANCH), BF16),
        grid=(b, nb, s // tq),
        in_specs=[
            pl.BlockSpec((1, tq, LANES), lambda bi, hp, qi: (bi, qi, col0 + hp)),
            pl.BlockSpec((1, s, LANES), lambda bi, hp, qi: (bi, 0, col0 + nb + hp)),
            pl.BlockSpec((1, s, LANES), lambda bi, hp, qi: (bi, 0, col0 + 2 * nb + hp)),
            pl.BlockSpec((1, tq, LANES), lambda bi, hp, qi: (bi, qi, col0 + 3 * nb + hp)),
            pl.BlockSpec((1, tq, LANES), lambda bi, hp, qi: (bi, qi, 0)),
            pl.BlockSpec((1, s, LANES), lambda bi, hp, qi: (bi, 0, 0)),
        ],
        out_specs=pl.BlockSpec((1, tq, LANES), lambda bi, hp, qi: (bi, qi, hp)),
        scratch_shapes=[pltpu.VMEM((HEADS_PER_STEP, s, 2 * LANES), BF16),
                        pltpu.VMEM((HEADS_PER_STEP, s, LANES), BF16)],
        compiler_params=pltpu.CompilerParams(
            dimension_semantics=("arbitrary", "arbitrary", "arbitrary"),
            vmem_limit_bytes=VMEM_LIMIT),
        name="fox_attn",
    )(proj3, proj3, proj3, proj3, qfeat, kfeat)


def _merge_kernel(ya_ref, yb_ref, g_ref, x_ref, mod_ref, bg_ref, wa_ref, wb_ref, wo_ref,
                  fg_ref, o_ref):
    d = x_ref.shape[1]
    y_a = _dot(ya_ref[...], wa_ref[...])
    y_b = _dot(yb_ref[...], wb_ref[...])
    gates = jax.nn.sigmoid(g_ref[...].astype(F32) + bg_ref[...])
    merged = gates[:, :d] * y_a + gates[:, d:] * y_b
    upd = _dot(merged.astype(BF16), wo_ref[...])
    xn = x_ref[...] + mod_ref[0, 2:3, :] * upd
    inv = lax.rsqrt(jnp.mean(xn * xn, axis=-1, keepdims=True) + EPS)
    o_ref[...] = (xn * inv) * fg_ref[...]


def _merge(ya, yb, proj, gcol, x2, mod3, b_gate, wa, wb, wo, final_g, seq, tm):
    m, d = x2.shape
    blocks_per_seq = seq // tm
    return pl.pallas_call(
        _merge_kernel,
        out_shape=jax.ShapeDtypeStruct((m, d), F32),
        grid=(m // tm,),
        in_specs=[
            pl.BlockSpec((tm, D_BRANCH), lambda i: (i, 0)),
            pl.BlockSpec((tm, D_BRANCH), lambda i: (i, 0)),
            pl.BlockSpec((tm, N_BRANCH * d), lambda i: (i, gcol)),
            pl.BlockSpec((tm, d), lambda i: (i, 0)),
            pl.BlockSpec((1, 3, d), lambda i: (i // blocks_per_seq, 0, 0)),
            pl.BlockSpec((1, N_BRANCH * d), lambda i: (0, 0)),
            pl.BlockSpec((D_BRANCH, d), lambda i: (0, 0)),
            pl.BlockSpec((D_BRANCH, d), lambda i: (0, 0)),
            pl.BlockSpec((d, d), lambda i: (0, 0)),
            pl.BlockSpec((1, d), lambda i: (0, 0)),
        ],
        out_specs=pl.BlockSpec((tm, d), lambda i: (i, 0)),
        compiler_params=pltpu.CompilerParams(
            dimension_semantics=("arbitrary",), vmem_limit_bytes=VMEM_LIMIT),
        name="merge",
    )(ya, yb, proj, x2, mod3, b_gate, wa, wb, wo, final_g)


def kernel(x, c, w_ada, b_ada, norm_g, w_in, b_forget, w_o_sb, w_o_fox, b_gate, w_out, final_g):
    b, s, d = x.shape
    n_qkvz = 8 * D_BRANCH
    assert w_ada.shape[0] == 1, "single-layer trunk only"
    assert w_in.shape[2] == n_qkvz + N_HEADS + N_BRANCH * d
    assert s % Q_BLOCK == 0 and d % LANES == 0
    qk_scale = HEAD_DIM ** -0.5 * LOG2E

    w = w_in[0]
    col_scale = jnp.ones((n_qkvz,), F32)
    col_scale = col_scale.at[0:D_BRANCH].set(qk_scale)
    col_scale = col_scale.at[4 * D_BRANCH:5 * D_BRANCH].set(qk_scale)
    w_main = jnp.concatenate(
        [w[:, :n_qkvz] * col_scale, w[:, n_qkvz + N_HEADS:]], axis=1).astype(BF16)
    w_f = jnp.repeat(w[:, n_qkvz:n_qkvz + N_HEADS], FEAT_LANES, axis=1).astype(BF16)
    bf_spread = jnp.repeat(b_forget[0], FEAT_LANES).reshape(1, LANES)

    x2 = x.reshape(b * s, d)
    mod3 = jnp.transpose(_mod(c, w_ada[0], b_ada[0]), (1, 0, 2))
    proj, f = _proj(x2, mod3, norm_g[0].reshape(1, d), w_main, w_f, s, tm=1024, tn=2048)
    proj3 = proj.reshape(b, s, proj.shape[1])
    qfeat, kfeat = _fprep(f.reshape(b, s, LANES), bf_spread)
    nb = D_BRANCH // LANES
    ya = _sb_attn(proj3, 0)
    yb = _fox_attn(proj3, qfeat, kfeat, 4 * nb)
    out = _merge(ya.reshape(b * s, D_BRANCH), yb.reshape(b * s, D_BRANCH), proj,
                 n_qkvz // (N_BRANCH * d), x2, mod3, b_gate[0].reshape(1, N_BRANCH * d),
                 w_o_sb[0].astype(BF16), w_o_fox[0].astype(BF16), w_out[0].astype(BF16),
                 final_g.reshape(1, d), s, tm=512)
    return out.reshape(b, s, d)
```

```python
import functools

import jax
import jax.numpy as jnp
from jax import lax
from jax.experimental import pallas as pl
from jax.experimental.pallas import tpu as pltpu

F32 = jnp.float32
BF16 = jnp.bfloat16

HEAD_DIM = 64
N_HEADS = 8
D_BRANCH = N_HEADS * HEAD_DIM
N_BRANCH = 2
EPS = 1e-6
NEG_BIG = -1e30
LOG2E = 1.4426950408889634
EXP2_CLAMP = 126.0

LANES = 128
HEADS_PER_STEP = LANES // HEAD_DIM
FEAT_LANES = LANES // N_HEADS
K_BLOCK = 256
Q_BLOCK = 4 * K_BLOCK

VMEM_LIMIT = 48 * 1024 * 1024


def _dot(a, b):
    return jnp.dot(a, b, preferred_element_type=F32)


def _dot_nt(a, b):
    return lax.dot_general(a, b, (((1,), (1,)), ((), ())), preferred_element_type=F32)


def _split2(x):
    hi = x.astype(BF16)
    lo = (x - hi.astype(F32)).astype(BF16)
    return hi, lo


def _split3(x):
    hi = x.astype(BF16)
    r1 = x - hi.astype(F32)
    mid = r1.astype(BF16)
    lo = (r1 - mid.astype(F32)).astype(BF16)
    return hi, mid, lo


def _log_sigmoid(x):
    return jnp.minimum(x, 0.0) - jnp.log(1.0 + jnp.exp(-jnp.abs(x)))


def _mod_kernel(c_ref, w_ref, b_ref, o_ref):
    c_hi, c_lo = _split2(c_ref[...])
    w_hi, w_lo = _split2(w_ref[...])
    acc = _dot(c_hi, w_hi) + _dot(c_hi, w_lo) + _dot(c_lo, w_hi)
    o_ref[0] = acc + b_ref[...]


def _mod(c, w_ada, b_ada):
    b, d = c.shape
    return pl.pallas_call(
        _mod_kernel,
        out_shape=jax.ShapeDtypeStruct((3, b, d), F32),
        grid=(3,),
        in_specs=[
            pl.BlockSpec((b, d), lambda j: (0, 0)),
            pl.BlockSpec((d, d), lambda j: (0, j)),
            pl.BlockSpec((1, d), lambda j: (0, j)),
        ],
        out_specs=pl.BlockSpec((1, b, d), lambda j: (j, 0, 0)),
        compiler_params=pltpu.CompilerParams(
            dimension_semantics=("arbitrary",), vmem_limit_bytes=VMEM_LIMIT),
        name="mod",
    )(c, w_ada, b_ada.reshape(1, 3 * d))


def _proj_kernel(x_ref, mod_ref, g_ref, w_ref, wf_ref, o_ref, f_ref, h_ref):
    @pl.when(pl.program_id(1) == 0)
    def _():
        x = x_ref[...]
        inv = lax.rsqrt(jnp.mean(x * x, axis=-1, keepdims=True) + EPS)
        shift = mod_ref[0, 0:1, :]
        gain = g_ref[...] * (1.0 + mod_ref[0, 1:2, :])
        h = (x * inv) * gain + shift
        hb = h.astype(BF16)
        h_ref[...] = hb
        f_ref[...] = _dot(hb, wf_ref[...])

    o_ref[...] = _dot(h_ref[...], w_ref[...]).astype(BF16)


def _proj(x2, mod3, norm_g, w_main, w_f, seq, tm, tn):
    m, d = x2.shape
    n = w_main.shape[1]
    blocks_per_seq = seq // tm
    return pl.pallas_call(
        _proj_kernel,
        out_shape=(jax.ShapeDtypeStruct((m, n), BF16),
                   jax.ShapeDtypeStruct((m, LANES), F32)),
        grid=(m // tm, n // tn),
        in_specs=[
            pl.BlockSpec((tm, d), lambda i, j: (i, 0)),
            pl.BlockSpec((1, 3, d), lambda i, j: (i // blocks_per_seq, 0, 0)),
            pl.BlockSpec((1, d), lambda i, j: (0, 0)),
            pl.BlockSpec((d, tn), lambda i, j: (0, j)),
            pl.BlockSpec((d, LANES), lambda i, j: (0, 0)),
        ],
        out_specs=(pl.BlockSpec((tm, tn), lambda i, j: (i, j)),
                   pl.BlockSpec((tm, LANES), lambda i, j: (i, 0))),
        scratch_shapes=[pltpu.VMEM((tm, d), BF16)],
        compiler_params=pltpu.CompilerParams(
            dimension_semantics=("arbitrary", "arbitrary"), vmem_limit_bytes=VMEM_LIMIT),
        name="proj",
    )(x2, mod3, norm_g, w_main, w_f)


def _fprep_kernel(f_ref, bf_ref, qf_ref, kf_ref, *, seq):
    blk = K_BLOCK
    row = lax.broadcasted_iota(jnp.int32, (blk, blk), 0)
    col = lax.broadcasted_iota(jnp.int32, (blk, blk), 1)
    tri = jnp.where(col <= row, 1.0, 0.0).astype(BF16)
    r = lax.broadcasted_iota(jnp.int32, (blk, LANES), 1) % FEAT_LANES
    carry = jnp.zeros((1, LANES), F32)
    for i in range(seq // blk):
        rows = pl.ds(i * blk, blk)
        lf = _log_sigmoid(f_ref[0, rows, :] + bf_ref[...])
        hi, mid, lo = _split3(lf)
        cum = (_dot(tri, hi) + _dot(tri, mid)) + _dot(tri, lo) + carry
        carry = cum[blk - 1:blk, :]
        c_hi, c_mid, c_lo = (p.astype(F32) for p in _split3(cum * LOG2E))
        qf = jnp.where(r == 0, c_hi, jnp.where(r == 1, c_mid, jnp.where(r == 2, c_lo,
             jnp.where(r < 6, 1.0, 0.0))))
        kf = jnp.where(r < 3, 1.0, jnp.where(r == 3, -c_hi, jnp.where(r == 4, -c_mid,
             jnp.where(r == 5, -c_lo, 0.0))))
        qf_ref[0, rows, :] = qf.astype(BF16)
        kf_ref[0, rows, :] = kf.astype(BF16)


def _fprep(f3, bf_spread):
    b, s, _ = f3.shape
    return pl.pallas_call(
        functools.partial(_fprep_kernel, seq=s),
        out_shape=(jax.ShapeDtypeStruct((b, s, LANES), BF16),
                   jax.ShapeDtypeStruct((b, s, LANES), BF16)),
        grid=(b,),
        in_specs=[pl.BlockSpec((1, s, LANES), lambda i: (i, 0, 0)),
                  pl.BlockSpec((1, LANES), lambda i: (0, 0))],
        out_specs=(pl.BlockSpec((1, s, LANES), lambda i: (i, 0, 0)),
                   pl.BlockSpec((1, s, LANES), lambda i: (i, 0, 0))),
        compiler_params=pltpu.CompilerParams(
            dimension_semantics=("arbitrary",), vmem_limit_bytes=VMEM_LIMIT),
        name="fprep",
    )(f3, bf_spread)


def _silu(z):
    return z * jax.nn.sigmoid(z)


def _head_mask(shape, head_in_step):
    lane = lax.broadcasted_iota(jnp.int32, shape, len(shape) - 1)
    return (lane // HEAD_DIM) == head_in_step


def _sb_kernel(q_ref, k_ref, v_ref, z_ref, o_ref, kx_ref, t_ref):
    tk = K_BLOCK
    tq = q_ref.shape[1]
    nsub = tq // tk
    qi = pl.program_id(2)

    @pl.when(qi == 0)
    def _init():
        k = k_ref[0].astype(F32)
        for h in range(HEADS_PER_STEP):
            kx_ref[h] = jnp.where(_head_mask(k.shape, h), k, 0.0).astype(BF16)
        j = lax.broadcasted_iota(jnp.int32, (tk, tk), 0)
        c = lax.broadcasted_iota(jnp.int32, (tk, tk), 1)
        t_ref[...] = jnp.where(j > c, 1.0, 0.0).astype(BF16)

    q = q_ref[0]
    row = lax.broadcasted_iota(jnp.int32, (tk, tk), 0)
    col = lax.broadcasted_iota(jnp.int32, (tk, tk), 1)
    strictly_lower = col < row

    def mask_top(x):
        top = jnp.where(strictly_lower, x[:tk], 0.0)
        return top if x.shape[0] == tk else jnp.concatenate([top, x[tk:]], axis=0)

    def tile(q_rows, h, key_start, carry, acc, on_diagonal):
        keys = pl.ds(key_start, tk)
        z = _dot_nt(q_rows, kx_ref[h, keys, :])
        lg = jnp.log(1.0 + jnp.exp2(jnp.minimum(z, EXP2_CLAMP))) * LOG2E
        sp = jnp.maximum(lg, z)
        ls = z - sp
        if on_diagonal:
            sp = mask_top(sp)
        sp_b = sp.astype(BF16)
        after = _dot(sp_b, t_ref[...]) + carry
        w = jnp.exp2(ls - after)
        if on_diagonal:
            w = mask_top(w)
        acc = acc + _dot(w.astype(BF16), v_ref[0, keys, :])
        return after[:, 0:1] + sp_b[:, 0:1].astype(F32), acc

    def sweep(qb):
        tiles = [(c, qb * nsub + c, True) for c in range(nsub - 1, -1, -1)]
        tiles += [(0, kb, False) for kb in range(qb * nsub - 1, -1, -1)]
        heads = range(HEADS_PER_STEP)
        carry = [[jnp.zeros((tk, 1), F32)] * nsub for _ in heads]
        acc = [[jnp.zeros((tk, LANES), F32)] * nsub for _ in heads]
        for r0, kb, on_diagonal in tiles:
            for h in heads:
                c_out, a_out = tile(q[r0 * tk:], h, kb * tk,
                                    jnp.concatenate(carry[h][r0:], axis=0),
                                    jnp.concatenate(acc[h][r0:], axis=0), on_diagonal)
                for r in range(r0, nsub):
                    rows = slice((r - r0) * tk, (r - r0 + 1) * tk)
                    carry[h][r], acc[h][r] = c_out[rows], a_out[rows]
        outs = [jnp.concatenate(acc[h], axis=0) for h in heads]
        y = jnp.where(_head_mask(outs[0].shape, 0), outs[0], outs[1])
        o_ref[0] = (y * _silu(z_ref[0].astype(F32))).astype(BF16)

    for qb in range(k_ref.shape[1] // tq):
        pl.when(qi == qb)(functools.partial(sweep, qb))


def _sb_attn(proj3, col0):
    b, s, _ = proj3.shape
    tq = Q_BLOCK
    nb = D_BRANCH // LANES
    return pl.pallas_call(
        _sb_kernel,
        out_shape=jax.ShapeDtypeStruct((b, s, D_BRANCH), BF16),
        grid=(b, nb, s // tq),
        in_specs=[
            pl.BlockSpec((1, tq, LANES), lambda bi, hp, qi: (bi, qi, col0 + hp)),
            pl.BlockSpec((1, s, LANES), lambda bi, hp, qi: (bi, 0, col0 + nb + hp)),
            pl.BlockSpec((1, s, LANES), lambda bi, hp, qi: (bi, 0, col0 + 2 * nb + hp)),
            pl.BlockSpec((1, tq, LANES), lambda bi, hp, qi: (bi, qi, col0 + 3 * nb + hp)),
        ],
        out_specs=pl.BlockSpec((1, tq, LANES), lambda bi, hp, qi: (bi, qi, hp)),
        scratch_shapes=[pltpu.VMEM((HEADS_PER_STEP, s, LANES), BF16),
                        pltpu.VMEM((K_BLOCK, K_BLOCK), BF16)],
        compiler_params=pltpu.CompilerParams(
            dimension_semantics=("arbitrary", "arbitrary", "arbitrary"),
            vmem_limit_bytes=VMEM_LIMIT),
        name="sb_attn",
    )(proj3, proj3, proj3, proj3)


def _fox_kernel(q_ref, k_ref, v_ref, z_ref, qf_ref, kf_ref, o_ref, kx_ref, vx_ref, zs_ref):
    tq = q_ref.shape[1]
    hp = pl.program_id(1)
    qi = pl.program_id(2)

    @pl.when(qi == 0)
    def _init():
        k = k_ref[0].astype(F32)
        kf = kf_ref[0].astype(F32)
        feat_lane = lax.broadcasted_iota(jnp.int32, kf.shape, 1) // FEAT_LANES
        for h in range(HEADS_PER_STEP):
            km = jnp.where(_head_mask(k.shape, h), k, 0.0)
            kfm = jnp.where(feat_lane == hp * HEADS_PER_STEP + h, kf, 0.0)
            kx_ref[h] = jnp.concatenate([km, kfm], axis=1).astype(BF16)
        v = v_ref[0].astype(F32)
        for h in range(HEADS_PER_STEP):
            vx_ref[h] = jnp.where(_head_mask(v.shape, h), v, 1.0).astype(BF16)

    for qb in range(k_ref.shape[1] // tq):
        pl.when(qi == qb)(functools.partial(
            _fox_sweep, qb, q_ref, z_ref, qf_ref, o_ref, kx_ref, vx_ref, zs_ref))


def _fox_sweep(qb, q_ref, z_ref, qf_ref, o_ref, kx_ref, vx_ref, zs_ref):
    tk = K_BLOCK
    tq = q_ref.shape[1]
    nsub = tq // tk
    row = lax.broadcasted_iota(jnp.int32, (tk, tk), 0)
    col = lax.broadcasted_iota(jnp.int32, (tk, tk), 1)
    causal = col <= row

    tiles = [(c, nsub - c, qb * nsub + c, True) for c in range(nsub)]
    tiles += [(0, nsub, kb, False) for kb in range(qb * nsub - 1, -1, -1)]

    qx = jnp.concatenate([q_ref[0], qf_ref[0]], axis=1)
    heads = range(HEADS_PER_STEP)
    zmax = [[None] * nsub for _ in heads]
    for r0, nr, kb, on_diagonal in tiles:
        rows, keys = pl.ds(r0 * tk, nr * tk), pl.ds(kb * tk, tk)
        for h in heads:
            z = _dot_nt(qx[r0 * tk:(r0 + nr) * tk], kx_ref[h, keys, :])
            if on_diagonal:
                top = jnp.where(causal, z[:tk], NEG_BIG)
                z = top if nr == 1 else jnp.concatenate([top, z[tk:]], axis=0)
            zs_ref[h, rows, keys] = z
            zm = jnp.maximum(z[:, :LANES], z[:, LANES:])
            for i in range(nr):
                part, old = zm[i * tk:(i + 1) * tk], zmax[h][r0 + i]
                zmax[h][r0 + i] = part if old is None else jnp.maximum(old, part)
    m = [[jnp.broadcast_to(jnp.max(zm, axis=1, keepdims=True), (tk, LANES)) for zm in zmax[h]]
         for h in heads]

    acc = [[jnp.zeros((tk, LANES), F32)] * nsub for _ in heads]
    for r0, nr, kb, _ in tiles:
        rows, keys = pl.ds(r0 * tk, nr * tk), pl.ds(kb * tk, tk)
        for h in heads:
            z = zs_ref[h, rows, keys]
            m_rows = m[h][r0] if nr == 1 else jnp.concatenate(m[h][r0:r0 + nr], axis=0)
            p = jnp.exp2(jnp.concatenate([z[:, :LANES] - m_rows, z[:, LANES:] - m_rows], axis=1))
            pv = _dot(p.astype(BF16), vx_ref[h, keys, :])
            for i in range(nr):
                acc[h][r0 + i] = acc[h][r0 + i] + pv[i * tk:(i + 1) * tk]
    outs = []
    for h in heads:
        a = jnp.concatenate(acc[h], axis=0)
        outs.append(a / pltpu.roll(a, HEAD_DIM, axis=1))
    y = jnp.where(_head_mask(outs[0].shape, 0), outs[0], outs[1])
    o_ref[0] = (y * _silu(z_ref[0].astype(F32))).astype(BF16)


def _fox_attn(proj3, qfeat, kfeat, col0):
    b, s, _ = proj3.shape
    tq = Q_BLOCK
    nb = D_BRANCH // LANES
    return pl.pallas_call(
        _fox_kernel,
        out_shape=jax.ShapeDtypeStruct((b, s, D_BRANCH), BF16),
        grid=(b, nb, s // tq),
        in_specs=[
            pl.BlockSpec((1, tq, LANES), lambda bi, hp, qi: (bi, qi, col0 + hp)),
            pl.BlockSpec((1, s, LANES), lambda bi, hp, qi: (bi, 0, col0 + nb + hp)),
            pl.BlockSpec((1, s, LANES), lambda bi, hp, qi: (bi, 0, col0 + 2 * nb + hp)),
            pl.BlockSpec((1, tq, LANES), lambda bi, hp, qi: (bi, qi, col0 + 3 * nb + hp)),
            pl.BlockSpec((1, tq, LANES), lambda bi, hp, qi: (bi, qi, 0)),
            pl.BlockSpec((1, s, LANES), lambda bi, hp, qi: (bi, 0, 0)),
        ],
        out_specs=pl.BlockSpec((1, tq, LANES), lambda bi, hp, qi: (bi, qi, hp)),
        scratch_shapes=[pltpu.VMEM((HEADS_PER_STEP, s, 2 * LANES), BF16),
                        pltpu.VMEM((HEADS_PER_STEP, s, LANES), BF16),
                        pltpu.VMEM((HEADS_PER_STEP, tq, s), F32)],
        compiler_params=pltpu.CompilerParams(
            dimension_semantics=("arbitrary", "arbitrary", "arbitrary"),
            vmem_limit_bytes=VMEM_LIMIT),
        name="fox_attn",
    )(proj3, proj3, proj3, proj3, qfeat, kfeat)


def _merge_kernel(ya_ref, yb_ref, g_ref, x_ref, mod_ref, bg_ref, wa_ref, wb_ref, wo_ref,
                  fg_ref, o_ref):
    d = x_ref.shape[1]
    y_a = _dot(ya_ref[...], wa_ref[...])
    y_b = _dot(yb_ref[...], wb_ref[...])
    gates = jax.nn.sigmoid(g_ref[...].astype(F32) + bg_ref[...])
    merged = gates[:, :d] * y_a + gates[:, d:] * y_b
    upd = _dot(merged.astype(BF16), wo_ref[...])
    xn = x_ref[...] + mod_ref[0, 2:3, :] * upd
    inv = lax.rsqrt(jnp.mean(xn * xn, axis=-1, keepdims=True) + EPS)
    o_ref[...] = (xn * inv) * fg_ref[...]


def _merge(ya, yb, proj, gcol, x2, mod3, b_gate, wa, wb, wo, final_g, seq, tm):
    m, d = x2.shape
    blocks_per_seq = seq // tm
    return pl.pallas_call(
        _merge_kernel,
        out_shape=jax.ShapeDtypeStruct((m, d), F32),
        grid=(m // tm,),
        in_specs=[
            pl.BlockSpec((tm, D_BRANCH), lambda i: (i, 0)),
            pl.BlockSpec((tm, D_BRANCH), lambda i: (i, 0)),
            pl.BlockSpec((tm, N_BRANCH * d), lambda i: (i, gcol)),
            pl.BlockSpec((tm, d), lambda i: (i, 0)),
            pl.BlockSpec((1, 3, d), lambda i: (i // blocks_per_seq, 0, 0)),
            pl.BlockSpec((1, N_BRANCH * d), lambda i: (0, 0)),
            pl.BlockSpec((D_BRANCH, d), lambda i: (0, 0)),
            pl.BlockSpec((D_BRANCH, d), lambda i: (0, 0)),
            pl.BlockSpec((d, d), lambda i: (0, 0)),
            pl.BlockSpec((1, d), lambda i: (0, 0)),
        ],
        out_specs=pl.BlockSpec((tm, d), lambda i: (i, 0)),
        compiler_params=pltpu.CompilerParams(
            dimension_semantics=("arbitrary",), vmem_limit_bytes=VMEM_LIMIT),
        name="merge",
    )(ya, yb, proj, x2, mod3, b_gate, wa, wb, wo, final_g)


def kernel(x, c, w_ada, b_ada, norm_g, w_in, b_forget, w_o_sb, w_o_fox, b_gate, w_out, final_g):
    b, s, d = x.shape
    n_qkvz = 8 * D_BRANCH
    assert w_ada.shape[0] == 1, "single-layer trunk only"
    assert w_in.shape[2] == n_qkvz + N_HEADS + N_BRANCH * d
    assert s % Q_BLOCK == 0 and d % LANES == 0
    qk_scale = HEAD_DIM ** -0.5 * LOG2E

    w = w_in[0]
    col_scale = jnp.ones((n_qkvz,), F32)
    col_scale = col_scale.at[0:D_BRANCH].set(qk_scale)
    col_scale = col_scale.at[4 * D_BRANCH:5 * D_BRANCH].set(qk_scale)
    w_main = jnp.concatenate(
        [w[:, :n_qkvz] * col_scale, w[:, n_qkvz + N_HEADS:]], axis=1).astype(BF16)
    w_f = jnp.repeat(w[:, n_qkvz:n_qkvz + N_HEADS], FEAT_LANES, axis=1).astype(BF16)
    bf_spread = jnp.repeat(b_forget[0], FEAT_LANES).reshape(1, LANES)

    x2 = x.reshape(b * s, d)
    mod3 = jnp.transpose(_mod(c, w_ada[0], b_ada[0]), (1, 0, 2))
    proj, f = _proj(x2, mod3, norm_g[0].reshape(1, d), w_main, w_f, s, tm=1024, tn=2048)
    proj3 = proj.reshape(b, s, proj.shape[1])
    qfeat, kfeat = _fprep(f.reshape(b, s, LANES), bf_spread)
    nb = D_BRANCH // LANES
    ya = _sb_attn(proj3, 0)
    yb = _fox_attn(proj3, qfeat, kfeat, 4 * nb)
    out = _merge(ya.reshape(b * s, D_BRANCH), yb.reshape(b * s, D_BRANCH), proj,
                 n_qkvz // (N_BRANCH * d), x2, mod3, b_gate[0].reshape(1, N_BRANCH * d),
                 w_o_sb[0].astype(BF16), w_o_fox[0].astype(BF16), w_out[0].astype(BF16),
                 final_g.reshape(1, d), s, tm=512)
    return out.reshape(b, s, d)
```

```python
import functools

import jax
import jax.numpy as jnp
from jax import lax
from jax.experimental import pallas as pl
from jax.experimental.pallas import tpu as pltpu

F32 = jnp.float32
BF16 = jnp.bfloat16

HEAD_DIM = 64
N_HEADS = 8
D_BRANCH = N_HEADS * HEAD_DIM
N_BRANCH = 2
EPS = 1e-6
NEG_BIG = -1e30
LOG2E = 1.4426950408889634
EXP2_CLAMP = 126.0
UNDERFLOW_CUT = 152.0

LANES = 128
HEADS_PER_STEP = LANES // HEAD_DIM
FEAT_LANES = LANES // N_HEADS
K_BLOCK = 256
Q_BLOCK = 4 * K_BLOCK
SB_Q_BLOCK = 8 * K_BLOCK

VMEM_LIMIT = 48 * 1024 * 1024


def _dot(a, b):
    return jnp.dot(a, b, preferred_element_type=F32)


def _dot_nt(a, b):
    return lax.dot_general(a, b, (((1,), (1,)), ((), ())), preferred_element_type=F32)


def _split2(x):
    hi = x.astype(BF16)
    lo = (x - hi.astype(F32)).astype(BF16)
    return hi, lo


def _split3(x):
    hi = x.astype(BF16)
    r1 = x - hi.astype(F32)
    mid = r1.astype(BF16)
    lo = (r1 - mid.astype(F32)).astype(BF16)
    return hi, mid, lo


def _log_sigmoid(x):
    return jnp.minimum(x, 0.0) - jnp.log(1.0 + jnp.exp(-jnp.abs(x)))


def _mod_kernel(c_ref, w_ref, b_ref, o_ref):
    c_hi, c_lo = _split2(c_ref[...])
    w_hi, w_lo = _split2(w_ref[...])
    acc = _dot(c_hi, w_hi) + _dot(c_hi, w_lo) + _dot(c_lo, w_hi)
    o_ref[0] = acc + b_ref[...]


def _mod(c, w_ada, b_ada):
    b, d = c.shape
    return pl.pallas_call(
        _mod_kernel,
        out_shape=jax.ShapeDtypeStruct((3, b, d), F32),
        grid=(3,),
        in_specs=[
            pl.BlockSpec((b, d), lambda j: (0, 0)),
            pl.BlockSpec((d, d), lambda j: (0, j)),
            pl.BlockSpec((1, d), lambda j: (0, j)),
        ],
        out_specs=pl.BlockSpec((1, b, d), lambda j: (j, 0, 0)),
        compiler_params=pltpu.CompilerParams(
            dimension_semantics=("arbitrary",), vmem_limit_bytes=VMEM_LIMIT),
        name="mod",
    )(c, w_ada, b_ada.reshape(1, 3 * d))


def _proj_kernel(x_ref, mod_ref, g_ref, w_ref, wg_ref, wf_ref, o_ref, f_ref, h_ref, *, n_qkvz_steps):
    j = pl.program_id(1)

    @pl.when(j == 0)
    def _():
        x = x_ref[...]
        inv = lax.rsqrt(jnp.mean(x * x, axis=-1, keepdims=True) + EPS)
        shift = mod_ref[0, 0:1, :]
        gain = g_ref[...] * (1.0 + mod_ref[0, 1:2, :])
        h = (x * inv) * gain + shift
        hb = h.astype(BF16)
        h_ref[...] = hb
        f_ref[...] = _dot(hb, wf_ref[...])

    @pl.when(j < n_qkvz_steps)
    def _():
        o_ref[...] = _dot(h_ref[...], w_ref[...]).astype(BF16)

    @pl.when(j >= n_qkvz_steps)
    def _():
        o_ref[...] = _dot(h_ref[...], wg_ref[...]).astype(BF16)


def _proj(x2, mod3, norm_g, w_qkvz, w_gate, w_f, seq, tm, tn):
    m, d = x2.shape
    n_qkvz_steps = w_qkvz.shape[1] // tn
    n = w_qkvz.shape[1] + w_gate.shape[1]
    assert w_gate.shape[1] == tn and w_qkvz.shape[1] % tn == 0
    blocks_per_seq = seq // tm
    return pl.pallas_call(
        functools.partial(_proj_kernel, n_qkvz_steps=n_qkvz_steps),
        out_shape=(jax.ShapeDtypeStruct((m, n), BF16),
                   jax.ShapeDtypeStruct((m, LANES), F32)),
        grid=(m // tm, n // tn),
        in_specs=[
            pl.BlockSpec((tm, d), lambda i, j: (i, 0)),
            pl.BlockSpec((1, 3, d), lambda i, j: (i // blocks_per_seq, 0, 0)),
            pl.BlockSpec((1, d), lambda i, j: (0, 0)),
            pl.BlockSpec((d, tn), lambda i, j: (0, jnp.minimum(j, n_qkvz_steps - 1))),
            pl.BlockSpec((d, tn), lambda i, j: (0, 0)),
            pl.BlockSpec((d, LANES), lambda i, j: (0, 0)),
        ],
        out_specs=(pl.BlockSpec((tm, tn), lambda i, j: (i, j)),
                   pl.BlockSpec((tm, LANES), lambda i, j: (i, 0))),
        scratch_shapes=[pltpu.VMEM((tm, d), BF16)],
        compiler_params=pltpu.CompilerParams(
            dimension_semantics=("arbitrary", "arbitrary"), vmem_limit_bytes=VMEM_LIMIT),
        name="proj",
    )(x2, mod3, norm_g, w_qkvz, w_gate, w_f)


def _fprep_kernel(f_ref, bf_ref, qf_ref, kf_ref, *, seq):
    blk = K_BLOCK
    row = lax.broadcasted_iota(jnp.int32, (blk, blk), 0)
    col = lax.broadcasted_iota(jnp.int32, (blk, blk), 1)
    tri = jnp.where(col <= row, 1.0, 0.0).astype(BF16)
    r = lax.broadcasted_iota(jnp.int32, (blk, LANES), 1) % FEAT_LANES
    carry = jnp.zeros((1, LANES), F32)
    for i in range(seq // blk):
        rows = pl.ds(i * blk, blk)
        lf = _log_sigmoid(f_ref[0, rows, :] + bf_ref[...])
        hi, mid, lo = _split3(lf)
        cum = (_dot(tri, hi) + _dot(tri, mid)) + _dot(tri, lo) + carry
        carry = cum[blk - 1:blk, :]
        c_hi, c_mid, c_lo = (p.astype(F32) for p in _split3(cum * LOG2E))
        qf = jnp.where(r == 0, c_hi, jnp.where(r == 1, c_mid, jnp.where(r == 2, c_lo,
             jnp.where(r < 6, 1.0, 0.0))))
        kf = jnp.where(r < 3, 1.0, jnp.where(r == 3, -c_hi, jnp.where(r == 4, -c_mid,
             jnp.where(r == 5, -c_lo, 0.0))))
        qf_ref[0, rows, :] = qf.astype(BF16)
        kf_ref[0, rows, :] = kf.astype(BF16)


def _fprep(f3, bf_spread):
    b, s, _ = f3.shape
    return pl.pallas_call(
        functools.partial(_fprep_kernel, seq=s),
        out_shape=(jax.ShapeDtypeStruct((b, s, LANES), BF16),
                   jax.ShapeDtypeStruct((b, s, LANES), BF16)),
        grid=(b,),
        in_specs=[pl.BlockSpec((1, s, LANES), lambda i: (i, 0, 0)),
                  pl.BlockSpec((1, LANES), lambda i: (0, 0))],
        out_specs=(pl.BlockSpec((1, s, LANES), lambda i: (i, 0, 0)),
                   pl.BlockSpec((1, s, LANES), lambda i: (i, 0, 0))),
        compiler_params=pltpu.CompilerParams(
            dimension_semantics=("arbitrary",), vmem_limit_bytes=VMEM_LIMIT),
        name="fprep",
    )(f3, bf_spread)


def _silu(z):
    return z * jax.nn.sigmoid(z)


def _head_mask(shape, head_in_step):
    lane = lax.broadcasted_iota(jnp.int32, shape, len(shape) - 1)
    return (lane // HEAD_DIM) == head_in_step


def _sb_kernel(q_ref, k_ref, v_ref, z_ref, o_ref, kx_ref, t_ref):
    tk = K_BLOCK
    tq = q_ref.shape[1]
    nsub = tq // tk
    qi = pl.program_id(2)

    @pl.when(qi == 0)
    def _init():
        k = k_ref[0].astype(F32)
        for h in range(HEADS_PER_STEP):
            kx_ref[h] = jnp.where(_head_mask(k.shape, h), k, 0.0).T.astype(BF16)
        j = lax.broadcasted_iota(jnp.int32, (tk, tk), 0)
        c = lax.broadcasted_iota(jnp.int32, (tk, tk), 1)
        t_ref[...] = jnp.where(j > c, 1.0, 0.0).astype(BF16)

    q = q_ref[0]
    row = lax.broadcasted_iota(jnp.int32, (tk, tk), 0)
    col = lax.broadcasted_iota(jnp.int32, (tk, tk), 1)
    strictly_lower = col < row

    def mask_top(x):
        top = jnp.where(strictly_lower, x[:tk], 0.0)
        return top if x.shape[0] == tk else jnp.concatenate([top, x[tk:]], axis=0)

    def tile(q_rows, h, key_start, carry, acc, on_diagonal):
        keys = pl.ds(key_start, tk)
        z = _dot(q_rows, kx_ref[h, :, keys])
        lg = jnp.log(1.0 + jnp.exp2(jnp.minimum(z, EXP2_CLAMP))) * LOG2E
        sp = jnp.maximum(lg, z)
        ls = z - sp
        if on_diagonal:
            sp = mask_top(sp)
        sp_b = sp.astype(BF16)
        after = _dot(sp_b, t_ref[...]) + carry
        w = jnp.exp2(ls - after)
        if on_diagonal:
            w = mask_top(w)
        acc = acc + _dot(w.astype(BF16), v_ref[0, keys, :])
        return after[:, 0:1] + sp_b[:, 0:1].astype(F32), acc

    heads = range(HEADS_PER_STEP)

    def run(tiles, carry, acc):
        carry, acc = [list(c) for c in carry], [list(a) for a in acc]
        for r0, nr, kb, on_diagonal in tiles:
            for h in heads:
                c_out, a_out = tile(q[r0 * tk:(r0 + nr) * tk], h, kb * tk,
                                    jnp.concatenate(carry[h][r0:r0 + nr], axis=0),
                                    jnp.concatenate(acc[h][r0:r0 + nr], axis=0), on_diagonal)
                for i in range(nr):
                    rows = slice(i * tk, (i + 1) * tk)
                    carry[h][r0 + i], acc[h][r0 + i] = c_out[rows], a_out[rows]
        return carry, acc

    def sweep(qb):
        first = qb * nsub
        carry = [[jnp.zeros((tk, 1), F32)] * nsub for _ in heads]
        acc = [[jnp.zeros((tk, LANES), F32)] * nsub for _ in heads]
        near = [(c, min(2, nsub - c), first + c, True) for c in range(nsub - 1, -1, -1)]
        if qb > 0:
            near.append((0, 1, first - 1, False))
        carry, acc = run(near, carry, acc)
        far = [(max(0, kb + 2 - first), nsub - max(0, kb + 2 - first), kb, False)
               for kb in range(first + nsub - 3, -1, -1)]
        least = functools.reduce(jnp.minimum, [c for per_head in carry for c in per_head])
        acc = lax.cond(jnp.min(least) < UNDERFLOW_CUT,
                       lambda: run(far, carry, acc)[1], lambda: acc)
        outs = [jnp.concatenate(acc[h], axis=0) for h in heads]
        y = jnp.where(_head_mask(outs[0].shape, 0), outs[0], outs[1])
        o_ref[0] = (y * _silu(z_ref[0].astype(F32))).astype(BF16)

    for qb in range(k_ref.shape[1] // tq):
        pl.when(qi == qb)(functools.partial(sweep, qb))


def _sb_attn(proj3, col0):
    b, s, _ = proj3.shape
    tq = SB_Q_BLOCK
    nb = D_BRANCH // LANES
    return pl.pallas_call(
        _sb_kernel,
        out_shape=jax.ShapeDtypeStruct((b, s, D_BRANCH), BF16),
        grid=(b, nb, s // tq),
        in_specs=[
            pl.BlockSpec((1, tq, LANES), lambda bi, hp, qi: (bi, qi, col0 + hp)),
            pl.BlockSpec((1, s, LANES), lambda bi, hp, qi: (bi, 0, col0 + nb + hp)),
            pl.BlockSpec((1, s, LANES), lambda bi, hp, qi: (bi, 0, col0 + 2 * nb + hp)),
            pl.BlockSpec((1, tq, LANES), lambda bi, hp, qi: (bi, qi, col0 + 3 * nb + hp)),
        ],
        out_specs=pl.BlockSpec((1, tq, LANES), lambda bi, hp, qi: (bi, qi, hp)),
        scratch_shapes=[pltpu.VMEM((HEADS_PER_STEP, LANES, s), BF16),
                        pltpu.VMEM((K_BLOCK, K_BLOCK), BF16)],
        compiler_params=pltpu.CompilerParams(
            dimension_semantics=("arbitrary", "arbitrary", "arbitrary"),
            vmem_limit_bytes=VMEM_LIMIT),
        name="sb_attn",
    )(proj3, proj3, proj3, proj3)


def _fox_kernel(q_ref, k_ref, v_ref, z_ref, qf_ref, kf_ref, o_ref, kx_ref, vx_ref, zs_ref):
    tq = q_ref.shape[1]
    hp = pl.program_id(1)
    qi = pl.program_id(2)

    @pl.when(qi == 0)
    def _init():
        k = k_ref[0].astype(F32)
        kf = kf_ref[0].astype(F32)
        feat_lane = lax.broadcasted_iota(jnp.int32, kf.shape, 1) // FEAT_LANES
        for h in range(HEADS_PER_STEP):
            km = jnp.where(_head_mask(k.shape, h), k, 0.0)
            kfm = jnp.where(feat_lane == hp * HEADS_PER_STEP + h, kf, 0.0)
            kx_ref[h] = jnp.concatenate([km.T, kfm.T], axis=0).astype(BF16)
        v = v_ref[0].astype(F32)
        for h in range(HEADS_PER_STEP):
            vx_ref[h] = jnp.where(_head_mask(v.shape, h), v, 1.0).astype(BF16)

    for qb in range(k_ref.shape[1] // tq):
        pl.when(qi == qb)(functools.partial(
            _fox_sweep, qb, q_ref, z_ref, qf_ref, o_ref, kx_ref, vx_ref, zs_ref))


def _fox_sweep(qb, q_ref, z_ref, qf_ref, o_ref, kx_ref, vx_ref, zs_ref):
    tk = K_BLOCK
    tq = q_ref.shape[1]
    nsub = tq // tk
    row = lax.broadcasted_iota(jnp.int32, (tk, tk), 0)
    col = lax.broadcasted_iota(jnp.int32, (tk, tk), 1)
    causal = col <= row

    tiles = [(c, nsub - c, qb * nsub + c, True) for c in range(nsub)]
    tiles += [(0, nsub, kb, False) for kb in range(qb * nsub - 1, -1, -1)]

    qx = jnp.concatenate([q_ref[0], qf_ref[0]], axis=1)
    heads = range(HEADS_PER_STEP)
    zmax = [[None] * nsub for _ in heads]
    for r0, nr, kb, on_diagonal in tiles:
        rows, keys = pl.ds(r0 * tk, nr * tk), pl.ds(kb * tk, tk)
        for h in heads:
            z = _dot(qx[r0 * tk:(r0 + nr) * tk], kx_ref[h, :, keys])
            if on_diagonal:
                top = jnp.where(causal, z[:tk], NEG_BIG)
                z = top if nr == 1 else jnp.concatenate([top, z[tk:]], axis=0)
            zs_ref[h, rows, keys] = z
            zm = jnp.maximum(z[:, :LANES], z[:, LANES:])
            for i in range(nr):
                part, old = zm[i * tk:(i + 1) * tk], zmax[h][r0 + i]
                zmax[h][r0 + i] = part if old is None else jnp.maximum(old, part)
    m = [[jnp.broadcast_to(jnp.max(zm, axis=1, keepdims=True), (tk, LANES)) for zm in zmax[h]]
         for h in heads]

    acc = [[jnp.zeros((tk, LANES), F32)] * nsub for _ in heads]
    for r0, nr, kb, _ in tiles:
        rows, keys = pl.ds(r0 * tk, nr * tk), pl.ds(kb * tk, tk)
        for h in heads:
            z = zs_ref[h, rows, keys]
            m_rows = m[h][r0] if nr == 1 else jnp.concatenate(m[h][r0:r0 + nr], axis=0)
            p = jnp.exp2(jnp.concatenate([z[:, :LANES] - m_rows, z[:, LANES:] - m_rows], axis=1))
            pv = _dot(p.astype(BF16), vx_ref[h, keys, :])
            for i in range(nr):
                acc[h][r0 + i] = acc[h][r0 + i] + pv[i * tk:(i + 1) * tk]
    outs = []
    for h in heads:
        a = jnp.concatenate(acc[h], axis=0)
        outs.append(a / pltpu.roll(a, HEAD_DIM, axis=1))
    y = jnp.where(_head_mask(outs[0].shape, 0), outs[0], outs[1])
    o_ref[0] = (y * _silu(z_ref[0].astype(F32))).astype(BF16)


def _fox_attn(proj3, qfeat, kfeat, col0):
    b, s, _ = proj3.shape
    tq = Q_BLOCK
    nb = D_BRANCH // LANES
    return pl.pallas_call(
        _fox_kernel,
        out_shape=jax.ShapeDtypeStruct((b, s, D_BRANCH), BF16),
        grid=(b, nb, s // tq),
        in_specs=[
            pl.BlockSpec((1, tq, LANES), lambda bi, hp, qi: (bi, qi, col0 + hp)),
            pl.BlockSpec((1, s, LANES), lambda bi, hp, qi: (bi, 0, col0 + nb + hp)),
            pl.BlockSpec((1, s, LANES), lambda bi, hp, qi: (bi, 0, col0 + 2 * nb + hp)),
            pl.BlockSpec((1, tq, LANES), lambda bi, hp, qi: (bi, qi, col0 + 3 * nb + hp)),
            pl.BlockSpec((1, tq, LANES), lambda bi, hp, qi: (bi, qi, 0)),
            pl.BlockSpec((1, s, LANES), lambda bi, hp, qi: (bi, 0, 0)),
        ],
        out_specs=pl.BlockSpec((1, tq, LANES), lambda bi, hp, qi: (bi, qi, hp)),
        scratch_shapes=[pltpu.VMEM((HEADS_PER_STEP, 2 * LANES, s), BF16),
                        pltpu.VMEM((HEADS_PER_STEP, s, LANES), BF16),
                        pltpu.VMEM((HEADS_PER_STEP, tq, s), F32)],
        compiler_params=pltpu.CompilerParams(
            dimension_semantics=("arbitrary", "arbitrary", "arbitrary"),
            vmem_limit_bytes=VMEM_LIMIT),
        name="fox_attn",
    )(proj3, proj3, proj3, proj3, qfeat, kfeat)


def _merge_kernel(ya_ref, yb_ref, g_ref, x_ref, mod_ref, bg_ref, wa_ref, wb_ref, wo_ref,
                  fg_ref, o_ref):
    d = x_ref.shape[1]
    y_a = _dot(ya_ref[...], wa_ref[...])
    y_b = _dot(yb_ref[...], wb_ref[...])
    gates = jax.nn.sigmoid(g_ref[...].astype(F32) + bg_ref[...])
    merged = gates[:, :d] * y_a + gates[:, d:] * y_b
    upd = _dot(merged.astype(BF16), wo_ref[...])
    xn = x_ref[...] + mod_ref[0, 2:3, :] * upd
    inv = lax.rsqrt(jnp.mean(xn * xn, axis=-1, keepdims=True) + EPS)
    o_ref[...] = (xn * inv) * fg_ref[...]


def _merge(ya, yb, proj, gcol, x2, mod3, b_gate, wa, wb, wo, final_g, seq, tm):
    m, d = x2.shape
    blocks_per_seq = seq // tm
    return pl.pallas_call(
        _merge_kernel,
        out_shape=jax.ShapeDtypeStruct((m, d), F32),
        grid=(m // tm,),
        in_specs=[
            pl.BlockSpec((tm, D_BRANCH), lambda i: (i, 0)),
            pl.BlockSpec((tm, D_BRANCH), lambda i: (i, 0)),
            pl.BlockSpec((tm, N_BRANCH * d), lambda i: (i, gcol)),
            pl.BlockSpec((tm, d), lambda i: (i, 0)),
            pl.BlockSpec((1, 3, d), lambda i: (i // blocks_per_seq, 0, 0)),
            pl.BlockSpec((1, N_BRANCH * d), lambda i: (0, 0)),
            pl.BlockSpec((D_BRANCH, d), lambda i: (0, 0)),
            pl.BlockSpec((D_BRANCH, d), lambda i: (0, 0)),
            pl.BlockSpec((d, d), lambda i: (0, 0)),
            pl.BlockSpec((1, d), lambda i: (0, 0)),
        ],
        out_specs=pl.BlockSpec((tm, d), lambda i: (i, 0)),
        compiler_params=pltpu.CompilerParams(
            dimension_semantics=("arbitrary",), vmem_limit_bytes=VMEM_LIMIT),
        name="merge",
    )(ya, yb, proj, x2, mod3, b_gate, wa, wb, wo, final_g)


def kernel(x, c, w_ada, b_ada, norm_g, w_in, b_forget, w_o_sb, w_o_fox, b_gate, w_out, final_g):
    b, s, d = x.shape
    n_qkvz = 8 * D_BRANCH
    assert w_ada.shape[0] == 1, "single-layer trunk only"
    assert w_in.shape[2] == n_qkvz + N_HEADS + N_BRANCH * d
    assert s % Q_BLOCK == 0 and s % SB_Q_BLOCK == 0 and d % LANES == 0
    qk_scale = HEAD_DIM ** -0.5 * LOG2E

    w = w_in[0]
    col_scale = jnp.ones((n_qkvz,), F32)
    col_scale = col_scale.at[0:D_BRANCH].set(qk_scale)
    col_scale = col_scale.at[4 * D_BRANCH:5 * D_BRANCH].set(qk_scale)
    w_qkvz = (w[:, :n_qkvz] * col_scale).astype(BF16)
    w_gate = w[:, n_qkvz + N_HEADS:].astype(BF16)
    w_f = jnp.repeat(w[:, n_qkvz:n_qkvz + N_HEADS], FEAT_LANES, axis=1).astype(BF16)
    bf_spread = jnp.repeat(b_forget[0], FEAT_LANES).reshape(1, LANES)

    x2 = x.reshape(b * s, d)
    mod3 = jnp.transpose(_mod(c, w_ada[0], b_ada[0]), (1, 0, 2))
    proj, f = _proj(x2, mod3, norm_g[0].reshape(1, d), w_qkvz, w_gate, w_f, s,
                    tm=1024, tn=N_BRANCH * d)
    proj3 = proj.reshape(b, s, proj.shape[1])
    qfeat, kfeat = _fprep(f.reshape(b, s, LANES), bf_spread)
    nb = D_BRANCH // LANES
    ya = _sb_attn(proj3, 0)
    yb = _fox_attn(proj3, qfeat, kfeat, 4 * nb)
    out = _merge(ya.reshape(b * s, D_BRANCH), yb.reshape(b * s, D_BRANCH), proj,
                 n_qkvz // (N_BRANCH * d), x2, mod3, b_gate[0].reshape(1, N_BRANCH * d),
                 w_o_sb[0].astype(BF16), w_o_fox[0].astype(BF16), w_out[0].astype(BF16),
                 final_g.reshape(1, d), s, tm=512)
    return out.reshape(b, s, d)
```

```python
import functools

import jax
import jax.numpy as jnp
from jax import lax
from jax.experimental import pallas as pl
from jax.experimental.pallas import tpu as pltpu

F32 = jnp.float32
BF16 = jnp.bfloat16

HEAD_DIM = 64
N_HEADS = 8
D_BRANCH = N_HEADS * HEAD_DIM
N_BRANCH = 2
EPS = 1e-6
NEG_BIG = -1e30
LOG2E = 1.4426950408889634
EXP2_CLAMP = 126.0
UNDERFLOW_CUT = 152.0

LANES = 128
HEADS_PER_STEP = LANES // HEAD_DIM
FEAT_LANES = LANES // N_HEADS
K_BLOCK = 256
Q_BLOCK = 4 * K_BLOCK
SB_Q_BLOCK = 8 * K_BLOCK
NEAR_BLOCKS = 3

VMEM_LIMIT = 48 * 1024 * 1024


def _dot(a, b):
    return jnp.dot(a, b, preferred_element_type=F32)


def _dot_nt(a, b):
    return lax.dot_general(a, b, (((1,), (1,)), ((), ())), preferred_element_type=F32)


def _split2(x):
    hi = x.astype(BF16)
    lo = (x - hi.astype(F32)).astype(BF16)
    return hi, lo


def _split3(x):
    hi = x.astype(BF16)
    r1 = x - hi.astype(F32)
    mid = r1.astype(BF16)
    lo = (r1 - mid.astype(F32)).astype(BF16)
    return hi, mid, lo


def _log_sigmoid(x):
    return jnp.minimum(x, 0.0) - jnp.log(1.0 + jnp.exp(-jnp.abs(x)))


def _mod_kernel(c_ref, w_ref, b_ref, o_ref):
    c_hi, c_lo = _split2(c_ref[...])
    w_hi, w_lo = _split2(w_ref[...])
    acc = _dot(c_hi, w_hi) + _dot(c_hi, w_lo) + _dot(c_lo, w_hi)
    o_ref[0] = acc + b_ref[...]


def _mod(c, w_ada, b_ada):
    b, d = c.shape
    return pl.pallas_call(
        _mod_kernel,
        out_shape=jax.ShapeDtypeStruct((3, b, d), F32),
        grid=(3,),
        in_specs=[
            pl.BlockSpec((b, d), lambda j: (0, 0)),
            pl.BlockSpec((d, d), lambda j: (0, j)),
            pl.BlockSpec((1, d), lambda j: (0, j)),
        ],
        out_specs=pl.BlockSpec((1, b, d), lambda j: (j, 0, 0)),
        compiler_params=pltpu.CompilerParams(
            dimension_semantics=("arbitrary",), vmem_limit_bytes=VMEM_LIMIT),
        name="mod",
    )(c, w_ada, b_ada.reshape(1, 3 * d))


def _proj_kernel(x_ref, mod_ref, g_ref, w_ref, wg_ref, wf_ref, o_ref, f_ref, h_ref, *, n_qkvz_steps):
    j = pl.program_id(1)

    @pl.when(j == 0)
    def _():
        x = x_ref[...]
        inv = lax.rsqrt(jnp.mean(x * x, axis=-1, keepdims=True) + EPS)
        shift = mod_ref[0, 0:1, :]
        gain = g_ref[...] * (1.0 + mod_ref[0, 1:2, :])
        h = (x * inv) * gain + shift
        hb = h.astype(BF16)
        h_ref[...] = hb
        f_ref[...] = _dot(hb, wf_ref[...])

    @pl.when(j < n_qkvz_steps)
    def _():
        o_ref[...] = _dot(h_ref[...], w_ref[...]).astype(BF16)

    @pl.when(j >= n_qkvz_steps)
    def _():
        o_ref[...] = _dot(h_ref[...], wg_ref[...]).astype(BF16)


def _proj(x2, mod3, norm_g, w_qkvz, w_gate, w_f, seq, tm, tn):
    m, d = x2.shape
    n_qkvz_steps = w_qkvz.shape[1] // tn
    n = w_qkvz.shape[1] + w_gate.shape[1]
    assert w_gate.shape[1] == tn and w_qkvz.shape[1] % tn == 0
    blocks_per_seq = seq // tm
    return pl.pallas_call(
        functools.partial(_proj_kernel, n_qkvz_steps=n_qkvz_steps),
        out_shape=(jax.ShapeDtypeStruct((m, n), BF16),
                   jax.ShapeDtypeStruct((m, LANES), F32)),
        grid=(m // tm, n // tn),
        in_specs=[
            pl.BlockSpec((tm, d), lambda i, j: (i, 0)),
            pl.BlockSpec((1, 3, d), lambda i, j: (i // blocks_per_seq, 0, 0)),
            pl.BlockSpec((1, d), lambda i, j: (0, 0)),
            pl.BlockSpec((d, tn), lambda i, j: (0, jnp.minimum(j, n_qkvz_steps - 1))),
            pl.BlockSpec((d, tn), lambda i, j: (0, 0)),
            pl.BlockSpec((d, LANES), lambda i, j: (0, 0)),
        ],
        out_specs=(pl.BlockSpec((tm, tn), lambda i, j: (i, j)),
                   pl.BlockSpec((tm, LANES), lambda i, j: (i, 0))),
        scratch_shapes=[pltpu.VMEM((tm, d), BF16)],
        compiler_params=pltpu.CompilerParams(
            dimension_semantics=("arbitrary", "arbitrary"), vmem_limit_bytes=VMEM_LIMIT),
        name="proj",
    )(x2, mod3, norm_g, w_qkvz, w_gate, w_f)


def _fprep_kernel(f_ref, bf_ref, qf_ref, kf_ref, *, seq):
    blk = K_BLOCK
    row = lax.broadcasted_iota(jnp.int32, (blk, blk), 0)
    col = lax.broadcasted_iota(jnp.int32, (blk, blk), 1)
    tri = jnp.where(col <= row, 1.0, 0.0).astype(BF16)
    r = lax.broadcasted_iota(jnp.int32, (blk, LANES), 1) % FEAT_LANES
    carry = jnp.zeros((1, LANES), F32)
    for i in range(seq // blk):
        rows = pl.ds(i * blk, blk)
        lf = _log_sigmoid(f_ref[0, rows, :] + bf_ref[...])
        hi, mid, lo = _split3(lf)
        cum = (_dot(tri, hi) + _dot(tri, mid)) + _dot(tri, lo) + carry
        carry = cum[blk - 1:blk, :]
        c_hi, c_mid, c_lo = (p.astype(F32) for p in _split3(cum * LOG2E))
        qf = jnp.where(r == 0, c_hi, jnp.where(r == 1, c_mid, jnp.where(r == 2, c_lo,
             jnp.where(r < 6, 1.0, 0.0))))
        kf = jnp.where(r < 3, 1.0, jnp.where(r == 3, -c_hi, jnp.where(r == 4, -c_mid,
             jnp.where(r == 5, -c_lo, 0.0))))
        qf_ref[0, rows, :] = qf.astype(BF16)
        kf_ref[0, rows, :] = kf.astype(BF16)


def _fprep(f3, bf_spread):
    b, s, _ = f3.shape
    return pl.pallas_call(
        functools.partial(_fprep_kernel, seq=s),
        out_shape=(jax.ShapeDtypeStruct((b, s, LANES), BF16),
                   jax.ShapeDtypeStruct((b, s, LANES), BF16)),
        grid=(b,),
        in_specs=[pl.BlockSpec((1, s, LANES), lambda i: (i, 0, 0)),
                  pl.BlockSpec((1, LANES), lambda i: (0, 0))],
        out_specs=(pl.BlockSpec((1, s, LANES), lambda i: (i, 0, 0)),
                   pl.BlockSpec((1, s, LANES), lambda i: (i, 0, 0))),
        compiler_params=pltpu.CompilerParams(
            dimension_semantics=("arbitrary",), vmem_limit_bytes=VMEM_LIMIT),
        name="fprep",
    )(f3, bf_spread)


def _silu(z):
    return z * jax.nn.sigmoid(z)


def _head_mask(shape, head_in_step):
    lane = lax.broadcasted_iota(jnp.int32, shape, len(shape) - 1)
    return (lane // HEAD_DIM) == head_in_step


def _sb_kernel(q_ref, k_ref, v_ref, z_ref, o_ref, kx_ref, t_ref):
    tk = K_BLOCK
    tq = q_ref.shape[1]
    nsub = tq // tk
    qi = pl.program_id(2)

    @pl.when(qi == 0)
    def _init():
        k = k_ref[0].astype(F32)
        for h in range(HEADS_PER_STEP):
            kx_ref[h] = jnp.where(_head_mask(k.shape, h), k, 0.0).T.astype(BF16)
        j = lax.broadcasted_iota(jnp.int32, (tk, tk), 0)
        c = lax.broadcasted_iota(jnp.int32, (tk, tk), 1)
        t_ref[...] = jnp.where(j > c, 1.0, 0.0).astype(BF16)

    q = q_ref[0]
    row = lax.broadcasted_iota(jnp.int32, (tk, tk), 0)
    col = lax.broadcasted_iota(jnp.int32, (tk, tk), 1)
    strictly_lower = col < row

    def mask_top(x):
        top = jnp.where(strictly_lower, x[:tk], 0.0)
        return top if x.shape[0] == tk else jnp.concatenate([top, x[tk:]], axis=0)

    def tile(q_rows, h, key_start, carry, acc, on_diagonal):
        keys = pl.ds(key_start, tk)
        z = _dot(q_rows, kx_ref[h, :, keys])
        lg = jnp.log(1.0 + jnp.exp2(jnp.minimum(z, EXP2_CLAMP))) * LOG2E
        sp = jnp.maximum(lg, z)
        ls = z - sp
        if on_diagonal:
            sp = mask_top(sp)
        sp_b = sp.astype(BF16)
        after = _dot(sp_b, t_ref[...]) + carry
        w = jnp.exp2(ls - after)
        if on_diagonal:
            w = mask_top(w)
        acc = acc + _dot(w.astype(BF16), v_ref[0, keys, :])
        return after[:, 0:1] + sp_b[:, 0:1].astype(F32), acc

    heads = range(HEADS_PER_STEP)

    def run(tiles, carry, acc):
        carry, acc = [list(c) for c in carry], [list(a) for a in acc]
        for r0, nr, kb, on_diagonal in tiles:
            for h in heads:
                c_out, a_out = tile(q[r0 * tk:(r0 + nr) * tk], h, kb * tk,
                                    jnp.concatenate(carry[h][r0:r0 + nr], axis=0),
                                    jnp.concatenate(acc[h][r0:r0 + nr], axis=0), on_diagonal)
                for i in range(nr):
                    rows = slice(i * tk, (i + 1) * tk)
                    carry[h][r0 + i], acc[h][r0 + i] = c_out[rows], a_out[rows]
        return carry, acc

    def sweep(qb):
        first = qb * nsub
        carry = [[jnp.zeros((tk, 1), F32)] * nsub for _ in heads]
        acc = [[jnp.zeros((tk, LANES), F32)] * nsub for _ in heads]
        near = []
        for kb in range(first + nsub - 1, max(first - NEAR_BLOCKS, -1), -1):
            r0, r1 = max(kb - first, 0), min(kb - first + NEAR_BLOCKS, nsub)
            near.append((r0, r1 - r0, kb, kb >= first))
        carry, acc = run(near, carry, acc)
        far = []
        for kb in range(first + nsub - 1 - NEAR_BLOCKS, -1, -1):
            r0 = max(kb + NEAR_BLOCKS - first, 0)
            far.append((r0, nsub - r0, kb, False))
        least = functools.reduce(jnp.minimum, [c for per_head in carry for c in per_head])
        acc = lax.cond(jnp.min(least) < UNDERFLOW_CUT,
                       lambda: run(far, carry, acc)[1], lambda: acc)
        outs = [jnp.concatenate(acc[h], axis=0) for h in heads]
        y = jnp.where(_head_mask(outs[0].shape, 0), outs[0], outs[1])
        o_ref[0] = (y * _silu(z_ref[0].astype(F32))).astype(BF16)

    for qb in range(k_ref.shape[1] // tq):
        pl.when(qi == qb)(functools.partial(sweep, qb))


def _sb_attn(proj3, col0):
    b, s, _ = proj3.shape
    tq = SB_Q_BLOCK
    nb = D_BRANCH // LANES
    return pl.pallas_call(
        _sb_kernel,
        out_shape=jax.ShapeDtypeStruct((b, s, D_BRANCH), BF16),
        grid=(b, nb, s // tq),
        in_specs=[
            pl.BlockSpec((1, tq, LANES), lambda bi, hp, qi: (bi, qi, col0 + hp)),
            pl.BlockSpec((1, s, LANES), lambda bi, hp, qi: (bi, 0, col0 + nb + hp)),
            pl.BlockSpec((1, s, LANES), lambda bi, hp, qi: (bi, 0, col0 + 2 * nb + hp)),
            pl.BlockSpec((1, tq, LANES), lambda bi, hp, qi: (bi, qi, col0 + 3 * nb + hp)),
        ],
        out_specs=pl.BlockSpec((1, tq, LANES), lambda bi, hp, qi: (bi, qi, hp)),
        scratch_shapes=[pltpu.VMEM((HEADS_PER_STEP, LANES, s), BF16),
                        pltpu.VMEM((K_BLOCK, K_BLOCK), BF16)],
        compiler_params=pltpu.CompilerParams(
            dimension_semantics=("arbitrary", "arbitrary", "arbitrary"),
            vmem_limit_bytes=VMEM_LIMIT),
        name="sb_attn",
    )(proj3, proj3, proj3, proj3)


def _fox_kernel(q_ref, k_ref, v_ref, z_ref, qf_ref, kf_ref, o_ref, kx_ref, vx_ref, zs_ref):
    tq = q_ref.shape[1]
    hp = pl.program_id(1)
    qi = pl.program_id(2)

    @pl.when(qi == 0)
    def _init():
        k = k_ref[0].astype(F32)
        kf = kf_ref[0].astype(F32)
        feat_lane = lax.broadcasted_iota(jnp.int32, kf.shape, 1) // FEAT_LANES
        for h in range(HEADS_PER_STEP):
            km = jnp.where(_head_mask(k.shape, h), k, 0.0)
            kfm = jnp.where(feat_lane == hp * HEADS_PER_STEP + h, kf, 0.0)
            kx_ref[h] = jnp.concatenate([km.T, kfm.T], axis=0).astype(BF16)
        v = v_ref[0].astype(F32)
        for h in range(HEADS_PER_STEP):
            vx_ref[h] = jnp.where(_head_mask(v.shape, h), v, 1.0).astype(BF16)

    for qb in range(k_ref.shape[1] // tq):
        pl.when(qi == qb)(functools.partial(
            _fox_sweep, qb, q_ref, z_ref, qf_ref, o_ref, kx_ref, vx_ref, zs_ref))


def _fox_sweep(qb, q_ref, z_ref, qf_ref, o_ref, kx_ref, vx_ref, zs_ref):
    tk = K_BLOCK
    tq = q_ref.shape[1]
    nsub = tq // tk
    row = lax.broadcasted_iota(jnp.int32, (tk, tk), 0)
    col = lax.broadcasted_iota(jnp.int32, (tk, tk), 1)
    causal = col <= row

    tiles = [(c, nsub - c, qb * nsub + c, True) for c in range(nsub)]
    tiles += [(0, nsub, kb, False) for kb in range(qb * nsub - 1, -1, -1)]

    qx = jnp.concatenate([q_ref[0], qf_ref[0]], axis=1)
    heads = range(HEADS_PER_STEP)
    zmax = [[None] * nsub for _ in heads]
    for r0, nr, kb, on_diagonal in tiles:
        rows, keys = pl.ds(r0 * tk, nr * tk), pl.ds(kb * tk, tk)
        for h in heads:
            z = _dot(qx[r0 * tk:(r0 + nr) * tk], kx_ref[h, :, keys])
            if on_diagonal:
                top = jnp.where(causal, z[:tk], NEG_BIG)
                z = top if nr == 1 else jnp.concatenate([top, z[tk:]], axis=0)
            zs_ref[h, rows, keys] = z
            zm = jnp.maximum(z[:, :LANES], z[:, LANES:])
            for i in range(nr):
                part, old = zm[i * tk:(i + 1) * tk], zmax[h][r0 + i]
                zmax[h][r0 + i] = part if old is None else jnp.maximum(old, part)
    m = [[jnp.broadcast_to(jnp.max(zm, axis=1, keepdims=True), (tk, LANES)) for zm in zmax[h]]
         for h in heads]

    acc = [[jnp.zeros((tk, LANES), F32)] * nsub for _ in heads]
    for r0, nr, kb, _ in tiles:
        rows, keys = pl.ds(r0 * tk, nr * tk), pl.ds(kb * tk, tk)
        for h in heads:
            z = zs_ref[h, rows, keys]
            m_rows = m[h][r0] if nr == 1 else jnp.concatenate(m[h][r0:r0 + nr], axis=0)
            p = jnp.exp2(jnp.concatenate([z[:, :LANES] - m_rows, z[:, LANES:] - m_rows], axis=1))
            pv = _dot(p.astype(BF16), vx_ref[h, keys, :])
            for i in range(nr):
                acc[h][r0 + i] = acc[h][r0 + i] + pv[i * tk:(i + 1) * tk]
    outs = []
    for h in heads:
        a = jnp.concatenate(acc[h], axis=0)
        outs.append(a / pltpu.roll(a, HEAD_DIM, axis=1))
    y = jnp.where(_head_mask(outs[0].shape, 0), outs[0], outs[1])
    o_ref[0] = (y * _silu(z_ref[0].astype(F32))).astype(BF16)


def _fox_attn(proj3, qfeat, kfeat, col0):
    b, s, _ = proj3.shape
    tq = Q_BLOCK
    nb = D_BRANCH // LANES
    return pl.pallas_call(
        _fox_kernel,
        out_shape=jax.ShapeDtypeStruct((b, s, D_BRANCH), BF16),
        grid=(b, nb, s // tq),
        in_specs=[
            pl.BlockSpec((1, tq, LANES), lambda bi, hp, qi: (bi, qi, col0 + hp)),
            pl.BlockSpec((1, s, LANES), lambda bi, hp, qi: (bi, 0, col0 + nb + hp)),
            pl.BlockSpec((1, s, LANES), lambda bi, hp, qi: (bi, 0, col0 + 2 * nb + hp)),
            pl.BlockSpec((1, tq, LANES), lambda bi, hp, qi: (bi, qi, col0 + 3 * nb + hp)),
            pl.BlockSpec((1, tq, LANES), lambda bi, hp, qi: (bi, qi, 0)),
            pl.BlockSpec((1, s, LANES), lambda bi, hp, qi: (bi, 0, 0)),
        ],
        out_specs=pl.BlockSpec((1, tq, LANES), lambda bi, hp, qi: (bi, qi, hp)),
        scratch_shapes=[pltpu.VMEM((HEADS_PER_STEP, 2 * LANES, s), BF16),
                        pltpu.VMEM((HEADS_PER_STEP, s, LANES), BF16),
                        pltpu.VMEM((HEADS_PER_STEP, tq, s), F32)],
        compiler_params=pltpu.CompilerParams(
            dimension_semantics=("arbitrary", "arbitrary", "arbitrary"),
            vmem_limit_bytes=VMEM_LIMIT),
        name="fox_attn",
    )(proj3, proj3, proj3, proj3, qfeat, kfeat)


def _merge_kernel(ya_ref, yb_ref, g_ref, x_ref, mod_ref, bg_ref, wa_ref, wb_ref, wo_ref,
                  fg_ref, o_ref):
    d = x_ref.shape[1]
    y_a = _dot(ya_ref[...], wa_ref[...])
    y_b = _dot(yb_ref[...], wb_ref[...])
    gates = jax.nn.sigmoid(g_ref[...].astype(F32) + bg_ref[...])
    merged = gates[:, :d] * y_a + gates[:, d:] * y_b
    upd = _dot(merged.astype(BF16), wo_ref[...])
    xn = x_ref[...] + mod_ref[0, 2:3, :] * upd
    inv = lax.rsqrt(jnp.mean(xn * xn, axis=-1, keepdims=True) + EPS)
    o_ref[...] = (xn * inv) * fg_ref[...]


def _merge(ya, yb, proj, gcol, x2, mod3, b_gate, wa, wb, wo, final_g, seq, tm):
    m, d = x2.shape
    blocks_per_seq = seq // tm
    return pl.pallas_call(
        _merge_kernel,
        out_shape=jax.ShapeDtypeStruct((m, d), F32),
        grid=(m // tm,),
        in_specs=[
            pl.BlockSpec((tm, D_BRANCH), lambda i: (i, 0)),
            pl.BlockSpec((tm, D_BRANCH), lambda i: (i, 0)),
            pl.BlockSpec((tm, N_BRANCH * d), lambda i: (i, gcol)),
            pl.BlockSpec((tm, d), lambda i: (i, 0)),
            pl.BlockSpec((1, 3, d), lambda i: (i // blocks_per_seq, 0, 0)),
            pl.BlockSpec((1, N_BRANCH * d), lambda i: (0, 0)),
            pl.BlockSpec((D_BRANCH, d), lambda i: (0, 0)),
            pl.BlockSpec((D_BRANCH, d), lambda i: (0, 0)),
            pl.BlockSpec((d, d), lambda i: (0, 0)),
            pl.BlockSpec((1, d), lambda i: (0, 0)),
        ],
        out_specs=pl.BlockSpec((tm, d), lambda i: (i, 0)),
        compiler_params=pltpu.CompilerParams(
            dimension_semantics=("arbitrary",), vmem_limit_bytes=VMEM_LIMIT),
        name="merge",
    )(ya, yb, proj, x2, mod3, b_gate, wa, wb, wo, final_g)


def kernel(x, c, w_ada, b_ada, norm_g, w_in, b_forget, w_o_sb, w_o_fox, b_gate, w_out, final_g):
    b, s, d = x.shape
    n_qkvz = 8 * D_BRANCH
    assert w_ada.shape[0] == 1, "single-layer trunk only"
    assert w_in.shape[2] == n_qkvz + N_HEADS + N_BRANCH * d
    assert s % Q_BLOCK == 0 and s % SB_Q_BLOCK == 0 and d % LANES == 0
    qk_scale = HEAD_DIM ** -0.5 * LOG2E

    w = w_in[0]
    col_scale = jnp.ones((n_qkvz,), F32)
    col_scale = col_scale.at[0:D_BRANCH].set(qk_scale)
    col_scale = col_scale.at[4 * D_BRANCH:5 * D_BRANCH].set(qk_scale)
    w_qkvz = (w[:, :n_qkvz] * col_scale).astype(BF16)
    w_gate = w[:, n_qkvz + N_HEADS:].astype(BF16)
    w_f = jnp.repeat(w[:, n_qkvz:n_qkvz + N_HEADS], FEAT_LANES, axis=1).astype(BF16)
    bf_spread = jnp.repeat(b_forget[0], FEAT_LANES).reshape(1, LANES)

    x2 = x.reshape(b * s, d)
    mod3 = jnp.transpose(_mod(c, w_ada[0], b_ada[0]), (1, 0, 2))
    proj, f = _proj(x2, mod3, norm_g[0].reshape(1, d), w_qkvz, w_gate, w_f, s,
                    tm=1024, tn=N_BRANCH * d)
    proj3 = proj.reshape(b, s, proj.shape[1])
    qfeat, kfeat = _fprep(f.reshape(b, s, LANES), bf_spread)
    nb = D_BRANCH // LANES
    ya = _sb_attn(proj3, 0)
    yb = _fox_attn(proj3, qfeat, kfeat, 4 * nb)
    out = _merge(ya.reshape(b * s, D_BRANCH), yb.reshape(b * s, D_BRANCH), proj,
                 n_qkvz // (N_BRANCH * d), x2, mod3, b_gate[0].reshape(1, N_BRANCH * d),
                 w_o_sb[0].astype(BF16), w_o_fox[0].astype(BF16), w_out[0].astype(BF16),
                 final_g.reshape(1, d), s, tm=512)
    return out.reshape(b, s, d)
```

```python
import functools

import jax
import jax.numpy as jnp
from jax import lax
from jax.experimental import pallas as pl
from jax.experimental.pallas import tpu as pltpu

F32 = jnp.float32
BF16 = jnp.bfloat16

HEAD_DIM = 64
N_HEADS = 8
D_BRANCH = N_HEADS * HEAD_DIM
N_BRANCH = 2
EPS = 1e-6
NEG_BIG = -1e30
LOG2E = 1.4426950408889634
EXP2_CLAMP = 126.0
UNDERFLOW_CUT = 152.0

LANES = 128
HEADS_PER_STEP = LANES // HEAD_DIM
FEAT_LANES = LANES // N_HEADS
K_BLOCK = 256
Q_BLOCK = 4 * K_BLOCK
SB_Q_BLOCK = 8 * K_BLOCK
NEAR_BLOCKS = 3

VMEM_LIMIT = 48 * 1024 * 1024


def _dot(a, b):
    return jnp.dot(a, b, preferred_element_type=F32)


def _dot_nt(a, b):
    return lax.dot_general(a, b, (((1,), (1,)), ((), ())), preferred_element_type=F32)


def _split2(x):
    hi = x.astype(BF16)
    lo = (x - hi.astype(F32)).astype(BF16)
    return hi, lo


def _split3(x):
    hi = x.astype(BF16)
    r1 = x - hi.astype(F32)
    mid = r1.astype(BF16)
    lo = (r1 - mid.astype(F32)).astype(BF16)
    return hi, mid, lo


def _log_sigmoid(x):
    return jnp.minimum(x, 0.0) - jnp.log(1.0 + jnp.exp(-jnp.abs(x)))


def _mod_kernel(c_ref, w_ref, b_ref, o_ref):
    c_hi, c_lo = _split2(c_ref[...])
    w_hi, w_lo = _split2(w_ref[...])
    acc = _dot(c_hi, w_hi) + _dot(c_hi, w_lo) + _dot(c_lo, w_hi)
    o_ref[0] = acc + b_ref[...]


def _mod(c, w_ada, b_ada):
    b, d = c.shape
    return pl.pallas_call(
        _mod_kernel,
        out_shape=jax.ShapeDtypeStruct((3, b, d), F32),
        grid=(3,),
        in_specs=[
            pl.BlockSpec((b, d), lambda j: (0, 0)),
            pl.BlockSpec((d, d), lambda j: (0, j)),
            pl.BlockSpec((1, d), lambda j: (0, j)),
        ],
        out_specs=pl.BlockSpec((1, b, d), lambda j: (j, 0, 0)),
        compiler_params=pltpu.CompilerParams(
            dimension_semantics=("arbitrary",), vmem_limit_bytes=VMEM_LIMIT),
        name="mod",
    )(c, w_ada, b_ada.reshape(1, 3 * d))


def _proj_kernel(x_ref, mod_ref, g_ref, w_ref, wg_ref, wf_ref, o_ref, f_ref, h_ref, *, n_qkvz_steps):
    j = pl.program_id(1)

    @pl.when(j == 0)
    def _():
        x = x_ref[...]
        inv = lax.rsqrt(jnp.mean(x * x, axis=-1, keepdims=True) + EPS)
        shift = mod_ref[0, 0:1, :]
        gain = g_ref[...] * (1.0 + mod_ref[0, 1:2, :])
        h = (x * inv) * gain + shift
        hb = h.astype(BF16)
        h_ref[...] = hb
        f_ref[...] = _dot(hb, wf_ref[...])

    @pl.when(j < n_qkvz_steps)
    def _():
        o_ref[...] = _dot(h_ref[...], w_ref[...]).astype(BF16)

    @pl.when(j >= n_qkvz_steps)
    def _():
        o_ref[...] = _dot(h_ref[...], wg_ref[...]).astype(BF16)


def _proj(x2, mod3, norm_g, w_qkvz, w_gate, w_f, seq, tm, tn):
    m, d = x2.shape
    n_qkvz_steps = w_qkvz.shape[1] // tn
    n = w_qkvz.shape[1] + w_gate.shape[1]
    assert w_gate.shape[1] == tn and w_qkvz.shape[1] % tn == 0
    blocks_per_seq = seq // tm
    return pl.pallas_call(
        functools.partial(_proj_kernel, n_qkvz_steps=n_qkvz_steps),
        out_shape=(jax.ShapeDtypeStruct((m, n), BF16),
                   jax.ShapeDtypeStruct((m, LANES), F32)),
        grid=(m // tm, n // tn),
        in_specs=[
            pl.BlockSpec((tm, d), lambda i, j: (i, 0)),
            pl.BlockSpec((1, 3, d), lambda i, j: (i // blocks_per_seq, 0, 0)),
            pl.BlockSpec((1, d), lambda i, j: (0, 0)),
            pl.BlockSpec((d, tn), lambda i, j: (0, jnp.minimum(j, n_qkvz_steps - 1))),
            pl.BlockSpec((d, tn), lambda i, j: (0, 0)),
            pl.BlockSpec((d, LANES), lambda i, j: (0, 0)),
        ],
        out_specs=(pl.BlockSpec((tm, tn), lambda i, j: (i, j)),
                   pl.BlockSpec((tm, LANES), lambda i, j: (i, 0))),
        scratch_shapes=[pltpu.VMEM((tm, d), BF16)],
        compiler_params=pltpu.CompilerParams(
            dimension_semantics=("arbitrary", "arbitrary"), vmem_limit_bytes=VMEM_LIMIT),
        name="proj",
    )(x2, mod3, norm_g, w_qkvz, w_gate, w_f)


def _fprep_kernel(f_ref, bf_ref, qf_ref, kf_ref, *, seq):
    blk = K_BLOCK
    row = lax.broadcasted_iota(jnp.int32, (blk, blk), 0)
    col = lax.broadcasted_iota(jnp.int32, (blk, blk), 1)
    tri = jnp.where(col <= row, 1.0, 0.0).astype(BF16)
    r = lax.broadcasted_iota(jnp.int32, (blk, LANES), 1) % FEAT_LANES
    carry = jnp.zeros((1, LANES), F32)
    for i in range(seq // blk):
        rows = pl.ds(i * blk, blk)
        lf = _log_sigmoid(f_ref[0, rows, :] + bf_ref[...])
        hi, mid, lo = _split3(lf)
        cum = (_dot(tri, hi) + _dot(tri, mid)) + _dot(tri, lo) + carry
        carry = cum[blk - 1:blk, :]
        c_hi, c_mid, c_lo = (p.astype(F32) for p in _split3(cum * LOG2E))
        qf = jnp.where(r == 0, c_hi, jnp.where(r == 1, c_mid, jnp.where(r == 2, c_lo,
             jnp.where(r < 6, 1.0, 0.0))))
        kf = jnp.where(r < 3, 1.0, jnp.where(r == 3, -c_hi, jnp.where(r == 4, -c_mid,
             jnp.where(r == 5, -c_lo, 0.0))))
        qf_ref[0, rows, :] = qf.astype(BF16)
        kf_ref[0, rows, :] = kf.astype(BF16)


def _fprep(f3, bf_spread):
    b, s, _ = f3.shape
    return pl.pallas_call(
        functools.partial(_fprep_kernel, seq=s),
        out_shape=(jax.ShapeDtypeStruct((b, s, LANES), BF16),
                   jax.ShapeDtypeStruct((b, s, LANES), BF16)),
        grid=(b,),
        in_specs=[pl.BlockSpec((1, s, LANES), lambda i: (i, 0, 0)),
                  pl.BlockSpec((1, LANES), lambda i: (0, 0))],
        out_specs=(pl.BlockSpec((1, s, LANES), lambda i: (i, 0, 0)),
                   pl.BlockSpec((1, s, LANES), lambda i: (i, 0, 0))),
        compiler_params=pltpu.CompilerParams(
            dimension_semantics=("arbitrary",), vmem_limit_bytes=VMEM_LIMIT),
        name="fprep",
    )(f3, bf_spread)


def _silu(z):
    return z * jax.nn.sigmoid(z)


def _head_mask(shape, head_in_step):
    lane = lax.broadcasted_iota(jnp.int32, shape, len(shape) - 1)
    return (lane // HEAD_DIM) == head_in_step


def _sb_kernel(q_ref, k_ref, v_ref, z_ref, o_ref, kx_ref, t_ref):
    tk = K_BLOCK
    tq = q_ref.shape[1]
    nsub = tq // tk
    qi = pl.program_id(2)

    @pl.when(qi == 0)
    def _init():
        k = k_ref[0].astype(F32)
        for h in range(HEADS_PER_STEP):
            kx_ref[h] = jnp.where(_head_mask(k.shape, h), k, 0.0).T.astype(BF16)
        j = lax.broadcasted_iota(jnp.int32, (tk, tk), 0)
        c = lax.broadcasted_iota(jnp.int32, (tk, tk), 1)
        t_ref[...] = jnp.where(j > c, 1.0, 0.0).astype(BF16)

    q = q_ref[0]
    row = lax.broadcasted_iota(jnp.int32, (tk, tk), 0)
    col = lax.broadcasted_iota(jnp.int32, (tk, tk), 1)
    strictly_lower = col < row

    def mask_top(x):
        top = jnp.where(strictly_lower, x[:tk], 0.0)
        return top if x.shape[0] == tk else jnp.concatenate([top, x[tk:]], axis=0)

    def tile(q_rows, h, key_start, carry, acc, on_diagonal):
        keys = pl.ds(key_start, tk)
        z = _dot(q_rows, kx_ref[h, :, keys])
        lg = jnp.log(1.0 + jnp.exp2(jnp.minimum(z, EXP2_CLAMP))) * LOG2E
        sp = jnp.maximum(lg, z)
        ls = z - sp
        if on_diagonal:
            sp = mask_top(sp)
        sp_b = sp.astype(BF16)
        after = _dot(sp_b, t_ref[...]) + carry
        w = jnp.exp2(ls - after)
        if on_diagonal:
            w = mask_top(w)
        acc = acc + _dot(w.astype(BF16), v_ref[0, keys, :])
        return after[:, 0:1] + sp_b[:, 0:1].astype(F32), acc

    heads = range(HEADS_PER_STEP)

    def run(tiles, carry, acc):
        carry, acc = [list(c) for c in carry], [list(a) for a in acc]
        for r0, nr, kb, on_diagonal in tiles:
            for h in heads:
                c_out, a_out = tile(q[r0 * tk:(r0 + nr) * tk], h, kb * tk,
                                    jnp.concatenate(carry[h][r0:r0 + nr], axis=0),
                                    jnp.concatenate(acc[h][r0:r0 + nr], axis=0), on_diagonal)
                for i in range(nr):
                    rows = slice(i * tk, (i + 1) * tk)
                    carry[h][r0 + i], acc[h][r0 + i] = c_out[rows], a_out[rows]
        return carry, acc

    def sweep(qb):
        first = qb * nsub
        carry = [[jnp.zeros((tk, 1), F32)] * nsub for _ in heads]
        acc = [[jnp.zeros((tk, LANES), F32)] * nsub for _ in heads]
        near = []
        for kb in range(first + nsub - 1, max(first - NEAR_BLOCKS, -1), -1):
            r0, r1 = max(kb - first, 0), min(kb - first + NEAR_BLOCKS, nsub)
            near.append((r0, r1 - r0, kb, kb >= first))
        carry, acc = run(near, carry, acc)
        far = []
        for kb in range(first + nsub - 1 - NEAR_BLOCKS, -1, -1):
            r0 = max(kb + NEAR_BLOCKS - first, 0)
            far.append((r0, nsub - r0, kb, False))
        with_far = range(max(NEAR_BLOCKS - first, 0), nsub)
        least = functools.reduce(jnp.minimum, [carry[h][r] for h in heads for r in with_far])
        acc = lax.cond(jnp.min(least) < UNDERFLOW_CUT,
                       lambda: run(far, carry, acc)[1], lambda: acc)
        outs = [jnp.concatenate(acc[h], axis=0) for h in heads]
        y = jnp.where(_head_mask(outs[0].shape, 0), outs[0], outs[1])
        o_ref[0] = (y * _silu(z_ref[0].astype(F32))).astype(BF16)

    for qb in range(k_ref.shape[1] // tq):
        pl.when(qi == qb)(functools.partial(sweep, qb))


def _sb_attn(proj3, col0):
    b, s, _ = proj3.shape
    tq = SB_Q_BLOCK
    nb = D_BRANCH // LANES
    return pl.pallas_call(
        _sb_kernel,
        out_shape=jax.ShapeDtypeStruct((b, s, D_BRANCH), BF16),
        grid=(b, nb, s // tq),
        in_specs=[
            pl.BlockSpec((1, tq, LANES), lambda bi, hp, qi: (bi, qi, col0 + hp)),
            pl.BlockSpec((1, s, LANES), lambda bi, hp, qi: (bi, 0, col0 + nb + hp)),
            pl.BlockSpec((1, s, LANES), lambda bi, hp, qi: (bi, 0, col0 + 2 * nb + hp)),
            pl.BlockSpec((1, tq, LANES), lambda bi, hp, qi: (bi, qi, col0 + 3 * nb + hp)),
        ],
        out_specs=pl.BlockSpec((1, tq, LANES), lambda bi, hp, qi: (bi, qi, hp)),
        scratch_shapes=[pltpu.VMEM((HEADS_PER_STEP, LANES, s), BF16),
                        pltpu.VMEM((K_BLOCK, K_BLOCK), BF16)],
        compiler_params=pltpu.CompilerParams(
            dimension_semantics=("arbitrary", "arbitrary", "arbitrary"),
            vmem_limit_bytes=VMEM_LIMIT),
        name="sb_attn",
    )(proj3, proj3, proj3, proj3)


def _fox_kernel(q_ref, k_ref, v_ref, z_ref, qf_ref, kf_ref, o_ref, kx_ref, vx_ref, zs_ref):
    tq = q_ref.shape[1]
    hp = pl.program_id(1)
    qi = pl.program_id(2)

    @pl.when(qi == 0)
    def _init():
        k = k_ref[0].astype(F32)
        kf = kf_ref[0].astype(F32)
        feat_lane = lax.broadcasted_iota(jnp.int32, kf.shape, 1) // FEAT_LANES
        for h in range(HEADS_PER_STEP):
            km = jnp.where(_head_mask(k.shape, h), k, 0.0)
            kfm = jnp.where(feat_lane == hp * HEADS_PER_STEP + h, kf, 0.0)
            kx_ref[h] = jnp.concatenate([km.T, kfm.T], axis=0).astype(BF16)
        v = v_ref[0].astype(F32)
        for h in range(HEADS_PER_STEP):
            vx_ref[h] = jnp.where(_head_mask(v.shape, h), v, 1.0).astype(BF16)

    for qb in range(k_ref.shape[1] // tq):
        pl.when(qi == qb)(functools.partial(
            _fox_sweep, qb, q_ref, z_ref, qf_ref, o_ref, kx_ref, vx_ref, zs_ref))


def _fox_sweep(qb, q_ref, z_ref, qf_ref, o_ref, kx_ref, vx_ref, zs_ref):
    tk = K_BLOCK
    tq = q_ref.shape[1]
    nsub = tq // tk
    row = lax.broadcasted_iota(jnp.int32, (tk, tk), 0)
    col = lax.broadcasted_iota(jnp.int32, (tk, tk), 1)
    causal = col <= row

    tiles = [(c, nsub - c, qb * nsub + c, True) for c in range(nsub)]
    tiles += [(0, nsub, kb, False) for kb in range(qb * nsub - 1, -1, -1)]

    qx = jnp.concatenate([q_ref[0], qf_ref[0]], axis=1)
    heads = range(HEADS_PER_STEP)
    zmax = [[None] * nsub for _ in heads]
    for r0, nr, kb, on_diagonal in tiles:
        rows, keys = pl.ds(r0 * tk, nr * tk), pl.ds(kb * tk, tk)
        for h in heads:
            z = _dot(qx[r0 * tk:(r0 + nr) * tk], kx_ref[h, :, keys])
            if on_diagonal:
                top = jnp.where(causal, z[:tk], NEG_BIG)
                z = top if nr == 1 else jnp.concatenate([top, z[tk:]], axis=0)
            zs_ref[h, rows, keys] = z
            zm = jnp.maximum(z[:, :LANES], z[:, LANES:])
            for i in range(nr):
                part, old = zm[i * tk:(i + 1) * tk], zmax[h][r0 + i]
                zmax[h][r0 + i] = part if old is None else jnp.maximum(old, part)
    m = [[jnp.broadcast_to(jnp.max(zm, axis=1, keepdims=True), (tk, LANES)) for zm in zmax[h]]
         for h in heads]

    acc = [[jnp.zeros((tk, LANES), F32)] * nsub for _ in heads]
    for r0, nr, kb, _ in tiles:
        rows, keys = pl.ds(r0 * tk, nr * tk), pl.ds(kb * tk, tk)
        for h in heads:
            z = zs_ref[h, rows, keys]
            m_rows = m[h][r0] if nr == 1 else jnp.concatenate(m[h][r0:r0 + nr], axis=0)
            p = jnp.exp2(jnp.concatenate([z[:, :LANES] - m_rows, z[:, LANES:] - m_rows], axis=1))
            pv = _dot(p.astype(BF16), vx_ref[h, keys, :])
            for i in range(nr):
                acc[h][r0 + i] = acc[h][r0 + i] + pv[i * tk:(i + 1) * tk]
    outs = []
    for h in heads:
        a = jnp.concatenate(acc[h], axis=0)
        outs.append(a / pltpu.roll(a, HEAD_DIM, axis=1))
    y = jnp.where(_head_mask(outs[0].shape, 0), outs[0], outs[1])
    o_ref[0] = (y * _silu(z_ref[0].astype(F32))).astype(BF16)


def _fox_attn(proj3, qfeat, kfeat, col0):
    b, s, _ = proj3.shape
    tq = Q_BLOCK
    nb = D_BRANCH // LANES
    return pl.pallas_call(
        _fox_kernel,
        out_shape=jax.ShapeDtypeStruct((b, s, D_BRANCH), BF16),
        grid=(b, nb, s // tq),
        in_specs=[
            pl.BlockSpec((1, tq, LANES), lambda bi, hp, qi: (bi, qi, col0 + hp)),
            pl.BlockSpec((1, s, LANES), lambda bi, hp, qi: (bi, 0, col0 + nb + hp)),
            pl.BlockSpec((1, s, LANES), lambda bi, hp, qi: (bi, 0, col0 + 2 * nb + hp)),
            pl.BlockSpec((1, tq, LANES), lambda bi, hp, qi: (bi, qi, col0 + 3 * nb + hp)),
            pl.BlockSpec((1, tq, LANES), lambda bi, hp, qi: (bi, qi, 0)),
            pl.BlockSpec((1, s, LANES), lambda bi, hp, qi: (bi, 0, 0)),
        ],
        out_specs=pl.BlockSpec((1, tq, LANES), lambda bi, hp, qi: (bi, qi, hp)),
        scratch_shapes=[pltpu.VMEM((HEADS_PER_STEP, 2 * LANES, s), BF16),
                        pltpu.VMEM((HEADS_PER_STEP, s, LANES), BF16),
                        pltpu.VMEM((HEADS_PER_STEP, tq, s), F32)],
        compiler_params=pltpu.CompilerParams(
            dimension_semantics=("arbitrary", "arbitrary", "arbitrary"),
            vmem_limit_bytes=VMEM_LIMIT),
        name="fox_attn",
    )(proj3, proj3, proj3, proj3, qfeat, kfeat)


def _merge_kernel(ya_ref, yb_ref, g_ref, x_ref, mod_ref, bg_ref, wa_ref, wb_ref, wo_ref,
                  fg_ref, o_ref):
    d = x_ref.shape[1]
    y_a = _dot(ya_ref[...], wa_ref[...])
    y_b = _dot(yb_ref[...], wb_ref[...])
    gates = jax.nn.sigmoid(g_ref[...].astype(F32) + bg_ref[...])
    merged = gates[:, :d] * y_a + gates[:, d:] * y_b
    upd = _dot(merged.astype(BF16), wo_ref[...])
    xn = x_ref[...] + mod_ref[0, 2:3, :] * upd
    inv = lax.rsqrt(jnp.mean(xn * xn, axis=-1, keepdims=True) + EPS)
    o_ref[...] = (xn * inv) * fg_ref[...]


def _merge(ya, yb, proj, gcol, x2, mod3, b_gate, wa, wb, wo, final_g, seq, tm):
    m, d = x2.shape
    blocks_per_seq = seq // tm
    return pl.pallas_call(
        _merge_kernel,
        out_shape=jax.ShapeDtypeStruct((m, d), F32),
        grid=(m // tm,),
        in_specs=[
            pl.BlockSpec((tm, D_BRANCH), lambda i: (i, 0)),
            pl.BlockSpec((tm, D_BRANCH), lambda i: (i, 0)),
            pl.BlockSpec((tm, N_BRANCH * d), lambda i: (i, gcol)),
            pl.BlockSpec((tm, d), lambda i: (i, 0)),
            pl.BlockSpec((1, 3, d), lambda i: (i // blocks_per_seq, 0, 0)),
            pl.BlockSpec((1, N_BRANCH * d), lambda i: (0, 0)),
            pl.BlockSpec((D_BRANCH, d), lambda i: (0, 0)),
            pl.BlockSpec((D_BRANCH, d), lambda i: (0, 0)),
            pl.BlockSpec((d, d), lambda i: (0, 0)),
            pl.BlockSpec((1, d), lambda i: (0, 0)),
        ],
        out_specs=pl.BlockSpec((tm, d), lambda i: (i, 0)),
        compiler_params=pltpu.CompilerParams(
            dimension_semantics=("arbitrary",), vmem_limit_bytes=VMEM_LIMIT),
        name="merge",
    )(ya, yb, proj, x2, mod3, b_gate, wa, wb, wo, final_g)


def kernel(x, c, w_ada, b_ada, norm_g, w_in, b_forget, w_o_sb, w_o_fox, b_gate, w_out, final_g):
    b, s, d = x.shape
    n_qkvz = 8 * D_BRANCH
    assert w_ada.shape[0] == 1, "single-layer trunk only"
    assert w_in.shape[2] == n_qkvz + N_HEADS + N_BRANCH * d
    assert s % Q_BLOCK == 0 and s % SB_Q_BLOCK == 0 and d % LANES == 0
    qk_scale = HEAD_DIM ** -0.5 * LOG2E

    w = w_in[0]
    col_scale = jnp.ones((n_qkvz,), F32)
    col_scale = col_scale.at[0:D_BRANCH].set(qk_scale)
    col_scale = col_scale.at[4 * D_BRANCH:5 * D_BRANCH].set(qk_scale)
    w_qkvz = (w[:, :n_qkvz] * col_scale).astype(BF16)
    w_gate = w[:, n_qkvz + N_HEADS:].astype(BF16)
    w_f = jnp.repeat(w[:, n_qkvz:n_qkvz + N_HEADS], FEAT_LANES, axis=1).astype(BF16)
    bf_spread = jnp.repeat(b_forget[0], FEAT_LANES).reshape(1, LANES)

    x2 = x.reshape(b * s, d)
    mod3 = jnp.transpose(_mod(c, w_ada[0], b_ada[0]), (1, 0, 2))
    proj, f = _proj(x2, mod3, norm_g[0].reshape(1, d), w_qkvz, w_gate, w_f, s,
                    tm=1024, tn=N_BRANCH * d)
    proj3 = proj.reshape(b, s, proj.shape[1])
    qfeat, kfeat = _fprep(f.reshape(b, s, LANES), bf_spread)
    nb = D_BRANCH // LANES
    ya = _sb_attn(proj3, 0)
    yb = _fox_attn(proj3, qfeat, kfeat, 4 * nb)
    out = _merge(ya.reshape(b * s, D_BRANCH), yb.reshape(b * s, D_BRANCH), proj,
                 n_qkvz // (N_BRANCH * d), x2, mod3, b_gate[0].reshape(1, N_BRANCH * d),
                 w_o_sb[0].astype(BF16), w_o_fox[0].astype(BF16), w_out[0].astype(BF16),
                 final_g.reshape(1, d), s, tm=512)
    return out.reshape(b, s, d)
```

```python
import functools

import jax
import jax.numpy as jnp
from jax import lax
from jax.experimental import pallas as pl
from jax.experimental.pallas import tpu as pltpu

F32 = jnp.float32
BF16 = jnp.bfloat16

HEAD_DIM = 64
N_HEADS = 8
D_BRANCH = N_HEADS * HEAD_DIM
N_BRANCH = 2
EPS = 1e-6
NEG_BIG = -1e30
LOG2E = 1.4426950408889634
EXP2_CLAMP = 126.0
UNDERFLOW_CUT = 152.0

LANES = 128
HEADS_PER_STEP = LANES // HEAD_DIM
FEAT_LANES = LANES // N_HEADS
K_BLOCK = 256
Q_BLOCK = 4 * K_BLOCK
SB_Q_BLOCK = 8 * K_BLOCK
NEAR_BLOCKS = 3

VMEM_LIMIT = 48 * 1024 * 1024


def _dot(a, b):
    return jnp.dot(a, b, preferred_element_type=F32)


def _dot_nt(a, b):
    return lax.dot_general(a, b, (((1,), (1,)), ((), ())), preferred_element_type=F32)


def _split2(x):
    hi = x.astype(BF16)
    lo = (x - hi.astype(F32)).astype(BF16)
    return hi, lo


def _split3(x):
    hi = x.astype(BF16)
    r1 = x - hi.astype(F32)
    mid = r1.astype(BF16)
    lo = (r1 - mid.astype(F32)).astype(BF16)
    return hi, mid, lo


def _log_sigmoid(x):
    return jnp.minimum(x, 0.0) - jnp.log(1.0 + jnp.exp(-jnp.abs(x)))


def _mod_kernel(c_ref, w_ref, b_ref, o_ref):
    c_hi, c_lo = _split2(c_ref[...])
    w_hi, w_lo = _split2(w_ref[...])
    acc = _dot(c_hi, w_hi) + _dot(c_hi, w_lo) + _dot(c_lo, w_hi)
    o_ref[0] = acc + b_ref[...]


def _mod(c, w_ada, b_ada):
    b, d = c.shape
    return pl.pallas_call(
        _mod_kernel,
        out_shape=jax.ShapeDtypeStruct((3, b, d), F32),
        grid=(3,),
        in_specs=[
            pl.BlockSpec((b, d), lambda j: (0, 0)),
            pl.BlockSpec((d, d), lambda j: (0, j)),
            pl.BlockSpec((1, d), lambda j: (0, j)),
        ],
        out_specs=pl.BlockSpec((1, b, d), lambda j: (j, 0, 0)),
        compiler_params=pltpu.CompilerParams(
            dimension_semantics=("arbitrary",), vmem_limit_bytes=VMEM_LIMIT),
        name="mod",
    )(c, w_ada, b_ada.reshape(1, 3 * d))


def _proj_kernel(x_ref, mod_ref, g_ref, w_ref, wg_ref, wf_ref, o_ref, f_ref, *, tn):
    x = x_ref[...]
    inv = lax.rsqrt(jnp.mean(x * x, axis=-1, keepdims=True) + EPS)
    shift = mod_ref[0, 0:1, :]
    gain = g_ref[...] * (1.0 + mod_ref[0, 1:2, :])
    hb = ((x * inv) * gain + shift).astype(BF16)
    f_ref[...] = _dot(hb, wf_ref[...])
    n_qkvz = w_ref.shape[1]
    for c in range(0, n_qkvz, tn):
        o_ref[:, c:c + tn] = _dot(hb, w_ref[:, c:c + tn]).astype(BF16)
    for c in range(0, wg_ref.shape[1], tn):
        o_ref[:, n_qkvz + c:n_qkvz + c + tn] = _dot(hb, wg_ref[:, c:c + tn]).astype(BF16)


def _proj(x2, mod3, norm_g, w_qkvz, w_gate, w_f, seq, tm, tn):
    m, d = x2.shape
    n = w_qkvz.shape[1] + w_gate.shape[1]
    assert w_qkvz.shape[1] % tn == 0 and w_gate.shape[1] % tn == 0
    blocks_per_seq = seq // tm
    resident = dict(pipeline_mode=pl.Buffered(1))
    return pl.pallas_call(
        functools.partial(_proj_kernel, tn=tn),
        out_shape=(jax.ShapeDtypeStruct((m, n), BF16),
                   jax.ShapeDtypeStruct((m, LANES), F32)),
        grid=(m // tm,),
        in_specs=[
            pl.BlockSpec((tm, d), lambda i: (i, 0)),
            pl.BlockSpec((1, 3, d), lambda i: (i // blocks_per_seq, 0, 0)),
            pl.BlockSpec((1, d), lambda i: (0, 0)),
            pl.BlockSpec(w_qkvz.shape, lambda i: (0, 0), **resident),
            pl.BlockSpec(w_gate.shape, lambda i: (0, 0), **resident),
            pl.BlockSpec((d, LANES), lambda i: (0, 0), **resident),
        ],
        out_specs=(pl.BlockSpec((tm, n), lambda i: (i, 0)),
                   pl.BlockSpec((tm, LANES), lambda i: (i, 0))),
        compiler_params=pltpu.CompilerParams(
            dimension_semantics=("arbitrary",), vmem_limit_bytes=VMEM_LIMIT),
        name="proj",
    )(x2, mod3, norm_g, w_qkvz, w_gate, w_f)


def _fprep_kernel(f_ref, bf_ref, qf_ref, kf_ref, *, seq):
    blk = K_BLOCK
    row = lax.broadcasted_iota(jnp.int32, (blk, blk), 0)
    col = lax.broadcasted_iota(jnp.int32, (blk, blk), 1)
    tri = jnp.where(col <= row, 1.0, 0.0).astype(BF16)
    r = lax.broadcasted_iota(jnp.int32, (blk, LANES), 1) % FEAT_LANES
    local = []
    for i in range(seq // blk):
        lf = _log_sigmoid(f_ref[0, pl.ds(i * blk, blk), :] + bf_ref[...])
        hi, mid, lo = _split3(lf)
        local.append((_dot(tri, hi) + _dot(tri, mid)) + _dot(tri, lo))
    offset = jnp.zeros((1, LANES), F32)
    for i in range(seq // blk):
        rows = pl.ds(i * blk, blk)
        cum = local[i] + offset
        offset = offset + local[i][blk - 1:blk, :]
        c_hi, c_mid, c_lo = (p.astype(F32) for p in _split3(cum * LOG2E))
        qf = jnp.where(r == 0, c_hi, jnp.where(r == 1, c_mid, jnp.where(r == 2, c_lo,
             jnp.where(r < 6, 1.0, 0.0))))
        kf = jnp.where(r < 3, 1.0, jnp.where(r == 3, -c_hi, jnp.where(r == 4, -c_mid,
             jnp.where(r == 5, -c_lo, 0.0))))
        qf_ref[0, rows, :] = qf.astype(BF16)
        kf_ref[0, rows, :] = kf.astype(BF16)


def _fprep(f3, bf_spread):
    b, s, _ = f3.shape
    return pl.pallas_call(
        functools.partial(_fprep_kernel, seq=s),
        out_shape=(jax.ShapeDtypeStruct((b, s, LANES), BF16),
                   jax.ShapeDtypeStruct((b, s, LANES), BF16)),
        grid=(b,),
        in_specs=[pl.BlockSpec((1, s, LANES), lambda i: (i, 0, 0)),
                  pl.BlockSpec((1, LANES), lambda i: (0, 0))],
        out_specs=(pl.BlockSpec((1, s, LANES), lambda i: (i, 0, 0)),
                   pl.BlockSpec((1, s, LANES), lambda i: (i, 0, 0))),
        compiler_params=pltpu.CompilerParams(
            dimension_semantics=("arbitrary",), vmem_limit_bytes=VMEM_LIMIT),
        name="fprep",
    )(f3, bf_spread)


def _silu(z):
    return z * jax.nn.sigmoid(z)


def _head_mask(shape, head_in_step):
    lane = lax.broadcasted_iota(jnp.int32, shape, len(shape) - 1)
    return (lane // HEAD_DIM) == head_in_step


def _sb_kernel(q_ref, k_ref, v_ref, z_ref, o_ref, kx_ref, t_ref):
    tk = K_BLOCK
    tq = q_ref.shape[1]
    nsub = tq // tk
    qi = pl.program_id(2)

    @pl.when(qi == 0)
    def _init():
        k = k_ref[0].astype(F32)
        for h in range(HEADS_PER_STEP):
            kx_ref[h] = jnp.where(_head_mask(k.shape, h), k, 0.0).T.astype(BF16)
        j = lax.broadcasted_iota(jnp.int32, (tk, tk), 0)
        c = lax.broadcasted_iota(jnp.int32, (tk, tk), 1)
        t_ref[...] = jnp.where(j > c, 1.0, 0.0).astype(BF16)

    q = q_ref[0]
    row = lax.broadcasted_iota(jnp.int32, (tk, tk), 0)
    col = lax.broadcasted_iota(jnp.int32, (tk, tk), 1)
    strictly_lower = col < row

    def mask_top(x):
        top = jnp.where(strictly_lower, x[:tk], 0.0)
        return top if x.shape[0] == tk else jnp.concatenate([top, x[tk:]], axis=0)

    def tile(q_rows, h, key_start, carry, acc, on_diagonal):
        keys = pl.ds(key_start, tk)
        z = _dot(q_rows, kx_ref[h, :, keys])
        lg = jnp.log(1.0 + jnp.exp2(jnp.minimum(z, EXP2_CLAMP))) * LOG2E
        sp = jnp.maximum(lg, z)
        ls = z - sp
        if on_diagonal:
            sp = mask_top(sp)
        sp_b = sp.astype(BF16)
        after = _dot(sp_b, t_ref[...]) + carry
        w = jnp.exp2(ls - after)
        if on_diagonal:
            w = mask_top(w)
        acc = acc + _dot(w.astype(BF16), v_ref[0, keys, :])
        return after[:, 0:1] + sp_b[:, 0:1].astype(F32), acc

    heads = range(HEADS_PER_STEP)

    def run(tiles, carry, acc):
        carry, acc = [list(c) for c in carry], [list(a) for a in acc]
        for r0, nr, kb, on_diagonal in tiles:
            for h in heads:
                c_out, a_out = tile(q[r0 * tk:(r0 + nr) * tk], h, kb * tk,
                                    jnp.concatenate(carry[h][r0:r0 + nr], axis=0),
                                    jnp.concatenate(acc[h][r0:r0 + nr], axis=0), on_diagonal)
                for i in range(nr):
                    rows = slice(i * tk, (i + 1) * tk)
                    carry[h][r0 + i], acc[h][r0 + i] = c_out[rows], a_out[rows]
        return carry, acc

    def sweep(qb):
        first = qb * nsub
        carry = [[jnp.zeros((tk, 1), F32)] * nsub for _ in heads]
        acc = [[jnp.zeros((tk, LANES), F32)] * nsub for _ in heads]
        n_blocks = first + nsub

        def band(lo, hi):
            tiles = []
            for kb in range(n_blocks - 1 - lo, -1, -1):
                r0, r1 = max(kb + lo - first, 0), min(kb + hi - first, nsub)
                if r0 < r1:
                    tiles.append((r0, r1 - r0, kb, lo == 0 and kb >= first))
            return tiles

        carry, acc = run(band(0, NEAR_BLOCKS), carry, acc)

        def rest(done, stages, carry, acc):
            rows_left = range(max(done - first, 0), nsub)
            if not stages or not rows_left:
                return acc
            least = functools.reduce(jnp.minimum, [carry[h][r] for h in heads for r in rows_left])

            def more():
                c2, a2 = run(band(done, done + stages[0]), carry, acc)
                return rest(done + stages[0], stages[1:], c2, a2)

            return lax.cond(jnp.min(least) < UNDERFLOW_CUT, more, lambda: acc)

        acc = rest(NEAR_BLOCKS, (1, n_blocks), carry, acc)
        outs = [jnp.concatenate(acc[h], axis=0) for h in heads]
        y = jnp.where(_head_mask(outs[0].shape, 0), outs[0], outs[1])
        o_ref[0] = (y * _silu(z_ref[0].astype(F32))).astype(BF16)

    for qb in range(k_ref.shape[1] // tq):
        pl.when(qi == qb)(functools.partial(sweep, qb))


def _sb_attn(proj3, col0):
    b, s, _ = proj3.shape
    tq = SB_Q_BLOCK
    nb = D_BRANCH // LANES
    return pl.pallas_call(
        _sb_kernel,
        out_shape=jax.ShapeDtypeStruct((b, s, D_BRANCH), BF16),
        grid=(b, nb, s // tq),
        in_specs=[
            pl.BlockSpec((1, tq, LANES), lambda bi, hp, qi: (bi, qi, col0 + hp)),
            pl.BlockSpec((1, s, LANES), lambda bi, hp, qi: (bi, 0, col0 + nb + hp)),
            pl.BlockSpec((1, s, LANES), lambda bi, hp, qi: (bi, 0, col0 + 2 * nb + hp)),
            pl.BlockSpec((1, tq, LANES), lambda bi, hp, qi: (bi, qi, col0 + 3 * nb + hp)),
        ],
        out_specs=pl.BlockSpec((1, tq, LANES), lambda bi, hp, qi: (bi, qi, hp)),
        scratch_shapes=[pltpu.VMEM((HEADS_PER_STEP, LANES, s), BF16),
                        pltpu.VMEM((K_BLOCK, K_BLOCK), BF16)],
        compiler_params=pltpu.CompilerParams(
            dimension_semantics=("arbitrary", "arbitrary", "arbitrary"),
            vmem_limit_bytes=VMEM_LIMIT),
        name="sb_attn",
    )(proj3, proj3, proj3, proj3)


def _fox_kernel(q_ref, k_ref, v_ref, z_ref, qf_ref, kf_ref, o_ref, kx_ref, vx_ref, zs_ref):
    tq = q_ref.shape[1]
    hp = pl.program_id(1)
    qi = pl.program_id(2)

    @pl.when(qi == 0)
    def _init():
        k = k_ref[0].astype(F32)
        kf = kf_ref[0].astype(F32)
        feat_lane = lax.broadcasted_iota(jnp.int32, kf.shape, 1) // FEAT_LANES
        for h in range(HEADS_PER_STEP):
            km = jnp.where(_head_mask(k.shape, h), k, 0.0)
            kfm = jnp.where(feat_lane == hp * HEADS_PER_STEP + h, kf, 0.0)
            kx_ref[h] = jnp.concatenate([km.T, kfm.T], axis=0).astype(BF16)
        v = v_ref[0].astype(F32)
        for h in range(HEADS_PER_STEP):
            vx_ref[h] = jnp.where(_head_mask(v.shape, h), v, 1.0).astype(BF16)

    for qb in range(k_ref.shape[1] // tq):
        pl.when(qi == qb)(functools.partial(
            _fox_sweep, qb, q_ref, z_ref, qf_ref, o_ref, kx_ref, vx_ref, zs_ref))


def _fox_sweep(qb, q_ref, z_ref, qf_ref, o_ref, kx_ref, vx_ref, zs_ref):
    tk = K_BLOCK
    tq = q_ref.shape[1]
    nsub = tq // tk
    row = lax.broadcasted_iota(jnp.int32, (tk, tk), 0)
    col = lax.broadcasted_iota(jnp.int32, (tk, tk), 1)
    causal = col <= row

    tiles = [(c, nsub - c, qb * nsub + c, True) for c in range(nsub)]
    tiles += [(0, nsub, kb, False) for kb in range(qb * nsub - 1, -1, -1)]

    qx = jnp.concatenate([q_ref[0], qf_ref[0]], axis=1)
    heads = range(HEADS_PER_STEP)
    zmax = [[None] * nsub for _ in heads]
    for r0, nr, kb, on_diagonal in tiles:
        rows, keys = pl.ds(r0 * tk, nr * tk), pl.ds(kb * tk, tk)
        for h in heads:
            z = _dot(qx[r0 * tk:(r0 + nr) * tk], kx_ref[h, :, keys])
            if on_diagonal:
                top = jnp.where(causal, z[:tk], NEG_BIG)
                z = top if nr == 1 else jnp.concatenate([top, z[tk:]], axis=0)
            zs_ref[h, rows, keys] = z
            zm = jnp.maximum(z[:, :LANES], z[:, LANES:])
            for i in range(nr):
                part, old = zm[i * tk:(i + 1) * tk], zmax[h][r0 + i]
                zmax[h][r0 + i] = part if old is None else jnp.maximum(old, part)
    m = [[jnp.broadcast_to(jnp.max(zm, axis=1, keepdims=True), (tk, LANES)) for zm in zmax[h]]
         for h in heads]

    acc = [[jnp.zeros((tk, LANES), F32)] * nsub for _ in heads]
    for r0, nr, kb, _ in tiles:
        rows, keys = pl.ds(r0 * tk, nr * tk), pl.ds(kb * tk, tk)
        for h in heads:
            z = zs_ref[h, rows, keys]
            m_rows = m[h][r0] if nr == 1 else jnp.concatenate(m[h][r0:r0 + nr], axis=0)
            p = jnp.exp2(jnp.concatenate([z[:, :LANES] - m_rows, z[:, LANES:] - m_rows], axis=1))
            pv = _dot(p.astype(BF16), vx_ref[h, keys, :])
            for i in range(nr):
                acc[h][r0 + i] = acc[h][r0 + i] + pv[i * tk:(i + 1) * tk]
    outs = []
    for h in heads:
        a = jnp.concatenate(acc[h], axis=0)
        outs.append(a / pltpu.roll(a, HEAD_DIM, axis=1))
    y = jnp.where(_head_mask(outs[0].shape, 0), outs[0], outs[1])
    o_ref[0] = (y * _silu(z_ref[0].astype(F32))).astype(BF16)


def _fox_attn(proj3, qfeat, kfeat, col0):
    b, s, _ = proj3.shape
    tq = Q_BLOCK
    nb = D_BRANCH // LANES
    return pl.pallas_call(
        _fox_kernel,
        out_shape=jax.ShapeDtypeStruct((b, s, D_BRANCH), BF16),
        grid=(b, nb, s // tq),
        in_specs=[
            pl.BlockSpec((1, tq, LANES), lambda bi, hp, qi: (bi, qi, col0 + hp)),
            pl.BlockSpec((1, s, LANES), lambda bi, hp, qi: (bi, 0, col0 + nb + hp)),
            pl.BlockSpec((1, s, LANES), lambda bi, hp, qi: (bi, 0, col0 + 2 * nb + hp)),
            pl.BlockSpec((1, tq, LANES), lambda bi, hp, qi: (bi, qi, col0 + 3 * nb + hp)),
            pl.BlockSpec((1, tq, LANES), lambda bi, hp, qi: (bi, qi, 0)),
            pl.BlockSpec((1, s, LANES), lambda bi, hp, qi: (bi, 0, 0)),
        ],
        out_specs=pl.BlockSpec((1, tq, LANES), lambda bi, hp, qi: (bi, qi, hp)),
        scratch_shapes=[pltpu.VMEM((HEADS_PER_STEP, 2 * LANES, s), BF16),
                        pltpu.VMEM((HEADS_PER_STEP, s, LANES), BF16),
                        pltpu.VMEM((HEADS_PER_STEP, tq, s), F32)],
        compiler_params=pltpu.CompilerParams(
            dimension_semantics=("arbitrary", "arbitrary", "arbitrary"),
            vmem_limit_bytes=VMEM_LIMIT),
        name="fox_attn",
    )(proj3, proj3, proj3, proj3, qfeat, kfeat)


def _merge_kernel(ya_ref, yb_ref, g_ref, x_ref, mod_ref, bg_ref, wa_ref, wb_ref, wo_ref,
                  fg_ref, o_ref):
    d = x_ref.shape[1]
    y_a = _dot(ya_ref[...], wa_ref[...])
    y_b = _dot(yb_ref[...], wb_ref[...])
    gates = jax.nn.sigmoid(g_ref[...].astype(F32) + bg_ref[...])
    merged = gates[:, :d] * y_a + gates[:, d:] * y_b
    upd = _dot(merged.astype(BF16), wo_ref[...])
    xn = x_ref[...] + mod_ref[0, 2:3, :] * upd
    inv = lax.rsqrt(jnp.mean(xn * xn, axis=-1, keepdims=True) + EPS)
    o_ref[...] = (xn * inv) * fg_ref[...]


def _merge(ya, yb, proj, gcol, x2, mod3, b_gate, wa, wb, wo, final_g, seq, tm):
    m, d = x2.shape
    blocks_per_seq = seq // tm
    return pl.pallas_call(
        _merge_kernel,
        out_shape=jax.ShapeDtypeStruct((m, d), F32),
        grid=(m // tm,),
        in_specs=[
            pl.BlockSpec((tm, D_BRANCH), lambda i: (i, 0)),
            pl.BlockSpec((tm, D_BRANCH), lambda i: (i, 0)),
            pl.BlockSpec((tm, N_BRANCH * d), lambda i: (i, gcol)),
            pl.BlockSpec((tm, d), lambda i: (i, 0)),
            pl.BlockSpec((1, 3, d), lambda i: (i // blocks_per_seq, 0, 0)),
            pl.BlockSpec((1, N_BRANCH * d), lambda i: (0, 0)),
            pl.BlockSpec((D_BRANCH, d), lambda i: (0, 0)),
            pl.BlockSpec((D_BRANCH, d), lambda i: (0, 0)),
            pl.BlockSpec((d, d), lambda i: (0, 0)),
            pl.BlockSpec((1, d), lambda i: (0, 0)),
        ],
        out_specs=pl.BlockSpec((tm, d), lambda i: (i, 0)),
        compiler_params=pltpu.CompilerParams(
            dimension_semantics=("arbitrary",), vmem_limit_bytes=VMEM_LIMIT),
        name="merge",
    )(ya, yb, proj, x2, mod3, b_gate, wa, wb, wo, final_g)


def kernel(x, c, w_ada, b_ada, norm_g, w_in, b_forget, w_o_sb, w_o_fox, b_gate, w_out, final_g):
    b, s, d = x.shape
    n_qkvz = 8 * D_BRANCH
    assert w_ada.shape[0] == 1, "single-layer trunk only"
    assert w_in.shape[2] == n_qkvz + N_HEADS + N_BRANCH * d
    assert s % Q_BLOCK == 0 and s % SB_Q_BLOCK == 0 and d % LANES == 0
    qk_scale = HEAD_DIM ** -0.5 * LOG2E

    w = w_in[0]
    col_scale = jnp.ones((n_qkvz,), F32)
    col_scale = col_scale.at[0:D_BRANCH].set(qk_scale)
    col_scale = col_scale.at[4 * D_BRANCH:5 * D_BRANCH].set(qk_scale)
    w_qkvz = (w[:, :n_qkvz] * col_scale).astype(BF16)
    w_gate = w[:, n_qkvz + N_HEADS:].astype(BF16)
    w_f = jnp.repeat(w[:, n_qkvz:n_qkvz + N_HEADS], FEAT_LANES, axis=1).astype(BF16)
    bf_spread = jnp.repeat(b_forget[0], FEAT_LANES).reshape(1, LANES)

    x2 = x.reshape(b * s, d)
    mod3 = jnp.transpose(_mod(c, w_ada[0], b_ada[0]), (1, 0, 2))
    proj, f = _proj(x2, mod3, norm_g[0].reshape(1, d), w_qkvz, w_gate, w_f, s,
                    tm=512, tn=1024)
    proj3 = proj.reshape(b, s, proj.shape[1])
    qfeat, kfeat = _fprep(f.reshape(b, s, LANES), bf_spread)
    nb = D_BRANCH // LANES
    ya = _sb_attn(proj3, 0)
    yb = _fox_attn(proj3, qfeat, kfeat, 4 * nb)
    out = _merge(ya.reshape(b * s, D_BRANCH), yb.reshape(b * s, D_BRANCH), proj,
                 n_qkvz // (N_BRANCH * d), x2, mod3, b_gate[0].reshape(1, N_BRANCH * d),
                 w_o_sb[0].astype(BF16), w_o_fox[0].astype(BF16), w_out[0].astype(BF16),
                 final_g.reshape(1, d), s, tm=512)
    return out.reshape(b, s, d)
```

```python
import functools

import jax
import jax.numpy as jnp
from jax import lax
from jax.experimental import pallas as pl
from jax.experimental.pallas import tpu as pltpu

F32 = jnp.float32
BF16 = jnp.bfloat16

HEAD_DIM = 64
N_HEADS = 8
D_BRANCH = N_HEADS * HEAD_DIM
N_BRANCH = 2
EPS = 1e-6
NEG_BIG = -1e30
LOG2E = 1.4426950408889634
EXP2_CLAMP = 126.0
UNDERFLOW_CUT = 152.0

LANES = 128
HEADS_PER_STEP = LANES // HEAD_DIM
FEAT_LANES = LANES // N_HEADS
K_BLOCK = 256
Q_BLOCK = 4 * K_BLOCK
SB_Q_BLOCK = 8 * K_BLOCK
NEAR_BLOCKS = 3
VMEM_LIMIT = 48 * 1024 * 1024


def _dot(a, b):
    return jnp.dot(a, b, preferred_element_type=F32)


def _dot_nt(a, b):
    return lax.dot_general(a, b, (((1,), (1,)), ((), ())), preferred_element_type=F32)


def _split2(x):
    hi = x.astype(BF16)
    lo = (x - hi.astype(F32)).astype(BF16)
    return hi, lo


def _split3(x):
    hi = x.astype(BF16)
    r1 = x - hi.astype(F32)
    mid = r1.astype(BF16)
    lo = (r1 - mid.astype(F32)).astype(BF16)
    return hi, mid, lo


def _log_sigmoid(x):
    return jnp.minimum(x, 0.0) - jnp.log(1.0 + jnp.exp(-jnp.abs(x)))


def _mod_kernel(c_ref, w_ref, b_ref, o_ref):
    c_hi, c_lo = _split2(c_ref[...])
    w_hi, w_lo = _split2(w_ref[...])
    acc = _dot(c_hi, w_hi) + _dot(c_hi, w_lo) + _dot(c_lo, w_hi)
    o_ref[0] = acc + b_ref[...]


def _mod(c, w_ada, b_ada):
    b, d = c.shape
    return pl.pallas_call(
        _mod_kernel,
        out_shape=jax.ShapeDtypeStruct((3, b, d), F32),
        grid=(3,),
        in_specs=[
            pl.BlockSpec((b, d), lambda j: (0, 0)),
            pl.BlockSpec((d, d), lambda j: (0, j)),
            pl.BlockSpec((1, d), lambda j: (0, j)),
        ],
        out_specs=pl.BlockSpec((1, b, d), lambda j: (j, 0, 0)),
        compiler_params=pltpu.CompilerParams(
            dimension_semantics=("arbitrary",), vmem_limit_bytes=VMEM_LIMIT),
        name="mod",
    )(c, w_ada, b_ada.reshape(1, 3 * d))


def _wprep_kernel(w_ref, nxt_ref, scale_ref, o_ref, *, n_plain, shift):
    j = pl.program_id(0)

    @pl.when(j < n_plain)
    def _():
        o_ref[...] = (w_ref[0] * scale_ref[...]).astype(BF16)

    @pl.when(j >= n_plain)
    def _():
        w = w_ref[0]
        width = w.shape[1]
        r = pltpu.roll(w, width - shift, axis=1)
        lane = lax.broadcasted_iota(jnp.int32, (w.shape[0], LANES), 1)
        tail = jnp.where(lane < LANES - shift, r[:, width - LANES:],
                         pltpu.roll(nxt_ref[0], LANES - shift, axis=1))
        o_ref[...] = jnp.concatenate([r[:, :width - LANES], tail], axis=1).astype(BF16)


def _wprep(w_in, col_scale, n_plain_cols, shift, n_out_cols, tn):
    _, d, _ = w_in.shape
    n_plain = n_plain_cols // tn
    return pl.pallas_call(
        functools.partial(_wprep_kernel, n_plain=n_plain, shift=shift),
        out_shape=jax.ShapeDtypeStruct((d, n_out_cols), BF16),
        grid=(n_out_cols // tn,),
        in_specs=[
            pl.BlockSpec((1, d, tn), lambda j: (0, 0, j)),
            pl.BlockSpec((1, d, LANES), lambda j: (0, 0, (j + 1) * (tn // LANES))),
            pl.BlockSpec((1, tn), lambda j: (0, jnp.minimum(j, n_plain - 1))),
        ],
        out_specs=pl.BlockSpec((d, tn), lambda j: (0, j)),
        compiler_params=pltpu.CompilerParams(
            dimension_semantics=("arbitrary",), vmem_limit_bytes=VMEM_LIMIT),
        name="wprep",
    )(w_in, w_in, col_scale.reshape(1, -1))


def _proj_kernel(x_ref, mod_ref, g_ref, w_ref, wf_ref, o_ref, f_ref, *, tn):
    x = x_ref[...]
    inv = lax.rsqrt(jnp.mean(x * x, axis=-1, keepdims=True) + EPS)
    shift = mod_ref[0, 0:1, :]
    gain = g_ref[...] * (1.0 + mod_ref[0, 1:2, :])
    hb = ((x * inv) * gain + shift).astype(BF16)
    f_ref[...] = _dot(hb, wf_ref[...])
    for c in range(0, w_ref.shape[1], tn):
        o_ref[:, c:c + tn] = _dot(hb, w_ref[:, c:c + tn]).astype(BF16)


def _proj(x2, mod3, norm_g, w_main, w_f, seq, tm, tn):
    m, d = x2.shape
    n = w_main.shape[1]
    assert n % tn == 0
    blocks_per_seq = seq // tm
    resident = dict(pipeline_mode=pl.Buffered(1))
    return pl.pallas_call(
        functools.partial(_proj_kernel, tn=tn),
        out_shape=(jax.ShapeDtypeStruct((m, n), BF16),
                   jax.ShapeDtypeStruct((m, LANES), F32)),
        grid=(m // tm,),
        in_specs=[
            pl.BlockSpec((tm, d), lambda i: (i, 0)),
            pl.BlockSpec((1, 3, d), lambda i: (i // blocks_per_seq, 0, 0)),
            pl.BlockSpec((1, d), lambda i: (0, 0)),
            pl.BlockSpec(w_main.shape, lambda i: (0, 0), **resident),
            pl.BlockSpec((d, LANES), lambda i: (0, 0), **resident),
        ],
        out_specs=(pl.BlockSpec((tm, n), lambda i: (i, 0)),
                   pl.BlockSpec((tm, LANES), lambda i: (i, 0))),
        compiler_params=pltpu.CompilerParams(
            dimension_semantics=("arbitrary",), vmem_limit_bytes=VMEM_LIMIT),
        name="proj",
    )(x2, mod3, norm_g, w_main, w_f)


def _fprep_kernel(f_ref, bf_ref, qf_ref, kf_ref, *, seq):
    blk = K_BLOCK
    row = lax.broadcasted_iota(jnp.int32, (blk, blk), 0)
    col = lax.broadcasted_iota(jnp.int32, (blk, blk), 1)
    tri = jnp.where(col <= row, 1.0, 0.0).astype(BF16)
    r = lax.broadcasted_iota(jnp.int32, (blk, LANES), 1) % FEAT_LANES
    local = []
    for i in range(seq // blk):
        lf = _log_sigmoid(f_ref[0, pl.ds(i * blk, blk), :] + bf_ref[...])
        hi, mid, lo = _split3(lf)
        local.append((_dot(tri, hi) + _dot(tri, mid)) + _dot(tri, lo))
    offset = jnp.zeros((1, LANES), F32)
    for i in range(seq // blk):
        rows = pl.ds(i * blk, blk)
        cum = local[i] + offset
        offset = offset + local[i][blk - 1:blk, :]
        c_hi, c_mid, c_lo = (p.astype(F32) for p in _split3(cum * LOG2E))
        qf = jnp.where(r == 0, c_hi, jnp.where(r == 1, c_mid, jnp.where(r == 2, c_lo,
             jnp.where(r < 6, 1.0, 0.0))))
        kf = jnp.where(r < 3, 1.0, jnp.where(r == 3, -c_hi, jnp.where(r == 4, -c_mid,
             jnp.where(r == 5, -c_lo, 0.0))))
        qf_ref[0, rows, :] = qf.astype(BF16)
        kf_ref[0, rows, :] = kf.astype(BF16)


def _fprep(f3, bf_spread):
    b, s, _ = f3.shape
    return pl.pallas_call(
        functools.partial(_fprep_kernel, seq=s),
        out_shape=(jax.ShapeDtypeStruct((b, s, LANES), BF16),
                   jax.ShapeDtypeStruct((b, s, LANES), BF16)),
        grid=(b,),
        in_specs=[pl.BlockSpec((1, s, LANES), lambda i: (i, 0, 0)),
                  pl.BlockSpec((1, LANES), lambda i: (0, 0))],
        out_specs=(pl.BlockSpec((1, s, LANES), lambda i: (i, 0, 0)),
                   pl.BlockSpec((1, s, LANES), lambda i: (i, 0, 0))),
        compiler_params=pltpu.CompilerParams(
            dimension_semantics=("arbitrary",), vmem_limit_bytes=VMEM_LIMIT),
        name="fprep",
    )(f3, bf_spread)


def _silu(z):
    return z * jax.nn.sigmoid(z)


def _head_mask(shape, head_in_step):
    lane = lax.broadcasted_iota(jnp.int32, shape, len(shape) - 1)
    return (lane // HEAD_DIM) == head_in_step


def _sb_kernel(q_ref, k_ref, v_ref, z_ref, o_ref, kx_ref, t_ref):
    tk = K_BLOCK
    tq = q_ref.shape[1]
    nsub = tq // tk
    qi = pl.program_id(2)

    @pl.when(qi == 0)
    def _init():
        k = k_ref[0].astype(F32)
        for h in range(HEADS_PER_STEP):
            kx_ref[h] = jnp.where(_head_mask(k.shape, h), k, 0.0).T.astype(BF16)
        j = lax.broadcasted_iota(jnp.int32, (tk, tk), 0)
        c = lax.broadcasted_iota(jnp.int32, (tk, tk), 1)
        t_ref[...] = jnp.where(j > c, 1.0, 0.0).astype(BF16)

    q = q_ref[0]
    row = lax.broadcasted_iota(jnp.int32, (tk, tk), 0)
    col = lax.broadcasted_iota(jnp.int32, (tk, tk), 1)
    strictly_lower = col < row

    def mask_top(x):
        top = jnp.where(strictly_lower, x[:tk], 0.0)
        return top if x.shape[0] == tk else jnp.concatenate([top, x[tk:]], axis=0)

    def tile(q_rows, h, key_start, carry, acc, on_diagonal):
        keys = pl.ds(key_start, tk)
        z = _dot(q_rows, kx_ref[h, :, keys])
        lg = jnp.log(1.0 + jnp.exp2(jnp.minimum(z, EXP2_CLAMP))) * LOG2E
        sp = jnp.maximum(lg, z)
        ls = z - sp
        if on_diagonal:
            sp = mask_top(sp)
        sp_b = sp.astype(BF16)
        after = _dot(sp_b, t_ref[...]) + carry
        w = jnp.exp2(ls - after)
        if on_diagonal:
            w = mask_top(w)
        acc = acc + _dot(w.astype(BF16), v_ref[0, keys, :])
        return after[:, 0:1] + sp_b[:, 0:1].astype(F32), acc

    heads = range(HEADS_PER_STEP)

    def run(tiles, carry, acc):
        carry, acc = [list(c) for c in carry], [list(a) for a in acc]
        for r0, nr, kb, on_diagonal in tiles:
            for h in heads:
                c_out, a_out = tile(q[r0 * tk:(r0 + nr) * tk], h, kb * tk,
                                    jnp.concatenate(carry[h][r0:r0 + nr], axis=0),
                                    jnp.concatenate(acc[h][r0:r0 + nr], axis=0), on_diagonal)
                for i in range(nr):
                    rows = slice(i * tk, (i + 1) * tk)
                    carry[h][r0 + i], acc[h][r0 + i] = c_out[rows], a_out[rows]
        return carry, acc

    def sweep(qb):
        first = qb * nsub
        carry = [[jnp.zeros((tk, 1), F32)] * nsub for _ in heads]
        acc = [[jnp.zeros((tk, LANES), F32)] * nsub for _ in heads]
        n_blocks = first + nsub

        def band(lo, hi):
            tiles = []
            for kb in range(n_blocks - 1 - lo, -1, -1):
                r0, r1 = max(kb + lo - first, 0), min(kb + hi - first, nsub)
                if r0 < r1:
                    tiles.append((r0, r1 - r0, kb, lo == 0 and kb >= first))
            return tiles

        carry, acc = run(band(0, NEAR_BLOCKS), carry, acc)

        def rest(done, stages, carry, acc):
            rows_left = range(max(done - first, 0), nsub)
            if not stages or not rows_left:
                return acc
            least = functools.reduce(jnp.minimum, [carry[h][r] for h in heads for r in rows_left])

            def more():
                c2, a2 = run(band(done, done + stages[0]), carry, acc)
                return rest(done + stages[0], stages[1:], c2, a2)

            return lax.cond(jnp.min(least) < UNDERFLOW_CUT, more, lambda: acc)

        acc = rest(NEAR_BLOCKS, (1, n_blocks), carry, acc)
        outs = [jnp.concatenate(acc[h], axis=0) for h in heads]
        y = jnp.where(_head_mask(outs[0].shape, 0), outs[0], outs[1])
        o_ref[0] = (y * _silu(z_ref[0].astype(F32))).astype(BF16)

    for qb in range(k_ref.shape[1] // tq):
        pl.when(qi == qb)(functools.partial(sweep, qb))


def _sb_attn(proj3, col0):
    b, s, _ = proj3.shape
    tq = SB_Q_BLOCK
    nb = D_BRANCH // LANES
    return pl.pallas_call(
        _sb_kernel,
        out_shape=jax.ShapeDtypeStruct((b, s, D_BRANCH), BF16),
        grid=(b, nb, s // tq),
        in_specs=[
            pl.BlockSpec((1, tq, LANES), lambda bi, hp, qi: (bi, qi, col0 + hp)),
            pl.BlockSpec((1, s, LANES), lambda bi, hp, qi: (bi, 0, col0 + nb + hp)),
            pl.BlockSpec((1, s, LANES), lambda bi, hp, qi: (bi, 0, col0 + 2 * nb + hp)),
            pl.BlockSpec((1, tq, LANES), lambda bi, hp, qi: (bi, qi, col0 + 3 * nb + hp)),
        ],
        out_specs=pl.BlockSpec((1, tq, LANES), lambda bi, hp, qi: (bi, qi, hp)),
        scratch_shapes=[pltpu.VMEM((HEADS_PER_STEP, LANES, s), BF16),
                        pltpu.VMEM((K_BLOCK, K_BLOCK), BF16)],
        compiler_params=pltpu.CompilerParams(
            dimension_semantics=("arbitrary", "arbitrary", "arbitrary"),
            vmem_limit_bytes=VMEM_LIMIT),
        name="sb_attn",
    )(proj3, proj3, proj3, proj3)


def _fox_kernel(q_ref, k_ref, v_ref, z_ref, qf_ref, kf_ref, o_ref, kx_ref, vx_ref, zs_ref):
    tq = q_ref.shape[1]
    hp = pl.program_id(1)
    qi = pl.program_id(2)

    @pl.when(qi == 0)
    def _init():
        k = k_ref[0].astype(F32)
        kf = kf_ref[0].astype(F32)
        feat_lane = lax.broadcasted_iota(jnp.int32, kf.shape, 1) // FEAT_LANES
        for h in range(HEADS_PER_STEP):
            km = jnp.where(_head_mask(k.shape, h), k, 0.0)
            kfm = jnp.where(feat_lane == hp * HEADS_PER_STEP + h, kf, 0.0)
            kx_ref[h] = jnp.concatenate([km.T, kfm.T], axis=0).astype(BF16)
        v = v_ref[0].astype(F32)
        for h in range(HEADS_PER_STEP):
            vx_ref[h] = jnp.where(_head_mask(v.shape, h), v, 1.0).astype(BF16)

    for qb in range(k_ref.shape[1] // tq):
        pl.when(qi == qb)(functools.partial(
            _fox_sweep, qb, q_ref, z_ref, qf_ref, o_ref, kx_ref, vx_ref, zs_ref))


def _fox_sweep(qb, q_ref, z_ref, qf_ref, o_ref, kx_ref, vx_ref, zs_ref):
    tk = K_BLOCK
    tq = q_ref.shape[1]
    nsub = tq // tk
    row = lax.broadcasted_iota(jnp.int32, (tk, tk), 0)
    col = lax.broadcasted_iota(jnp.int32, (tk, tk), 1)
    causal = col <= row

    tiles = [(c, nsub - c, qb * nsub + c, True) for c in range(nsub)]
    tiles += [(0, nsub, kb, False) for kb in range(qb * nsub - 1, -1, -1)]

    qx = jnp.concatenate([q_ref[0], qf_ref[0]], axis=1)
    heads = range(HEADS_PER_STEP)
    zmax = [[None] * nsub for _ in heads]
    for r0, nr, kb, on_diagonal in tiles:
        rows, keys = pl.ds(r0 * tk, nr * tk), pl.ds(kb * tk, tk)
        for h in heads:
            z = _dot(qx[r0 * tk:(r0 + nr) * tk], kx_ref[h, :, keys])
            if on_diagonal:
                top = jnp.where(causal, z[:tk], NEG_BIG)
                z = top if nr == 1 else jnp.concatenate([top, z[tk:]], axis=0)
            zs_ref[h, rows, keys] = z
            zm = jnp.maximum(z[:, :LANES], z[:, LANES:])
            for i in range(nr):
                part, old = zm[i * tk:(i + 1) * tk], zmax[h][r0 + i]
                zmax[h][r0 + i] = part if old is None else jnp.maximum(old, part)
    m = [[jnp.broadcast_to(jnp.max(zm, axis=1, keepdims=True), (tk, LANES)) for zm in zmax[h]]
         for h in heads]

    acc = [[jnp.zeros((tk, LANES), F32)] * nsub for _ in heads]
    for r0, nr, kb, _ in tiles:
        rows, keys = pl.ds(r0 * tk, nr * tk), pl.ds(kb * tk, tk)
        for h in heads:
            z = zs_ref[h, rows, keys]
            m_rows = m[h][r0] if nr == 1 else jnp.concatenate(m[h][r0:r0 + nr], axis=0)
            p = jnp.exp2(jnp.concatenate([z[:, :LANES] - m_rows, z[:, LANES:] - m_rows], axis=1))
            pv = _dot(p.astype(BF16), vx_ref[h, keys, :])
            for i in range(nr):
                acc[h][r0 + i] = acc[h][r0 + i] + pv[i * tk:(i + 1) * tk]
    outs = []
    for h in heads:
        a = jnp.concatenate(acc[h], axis=0)
        outs.append(a / pltpu.roll(a, HEAD_DIM, axis=1))
    y = jnp.where(_head_mask(outs[0].shape, 0), outs[0], outs[1])
    o_ref[0] = (y * _silu(z_ref[0].astype(F32))).astype(BF16)


def _fox_attn(proj3, qfeat, kfeat, col0):
    b, s, _ = proj3.shape
    tq = Q_BLOCK
    nb = D_BRANCH // LANES
    return pl.pallas_call(
        _fox_kernel,
        out_shape=jax.ShapeDtypeStruct((b, s, D_BRANCH), BF16),
        grid=(b, nb, s // tq),
        in_specs=[
            pl.BlockSpec((1, tq, LANES), lambda bi, hp, qi: (bi, qi, col0 + hp)),
            pl.BlockSpec((1, s, LANES), lambda bi, hp, qi: (bi, 0, col0 + nb + hp)),
            pl.BlockSpec((1, s, LANES), lambda bi, hp, qi: (bi, 0, col0 + 2 * nb + hp)),
            pl.BlockSpec((1, tq, LANES), lambda bi, hp, qi: (bi, qi, col0 + 3 * nb + hp)),
            pl.BlockSpec((1, tq, LANES), lambda bi, hp, qi: (bi, qi, 0)),
            pl.BlockSpec((1, s, LANES), lambda bi, hp, qi: (bi, 0, 0)),
        ],
        out_specs=pl.BlockSpec((1, tq, LANES), lambda bi, hp, qi: (bi, qi, hp)),
        scratch_shapes=[pltpu.VMEM((HEADS_PER_STEP, 2 * LANES, s), BF16),
                        pltpu.VMEM((HEADS_PER_STEP, s, LANES), BF16),
                        pltpu.VMEM((HEADS_PER_STEP, tq, s), F32)],
        compiler_params=pltpu.CompilerParams(
            dimension_semantics=("arbitrary", "arbitrary", "arbitrary"),
            vmem_limit_bytes=VMEM_LIMIT),
        name="fox_attn",
    )(proj3, proj3, proj3, proj3, qfeat, kfeat)


def _merge_kernel(ya_ref, yb_ref, g_ref, x_ref, mod_ref, bg_ref, wa_ref, wb_ref, wo_ref,
                  fg_ref, o_ref):
    d = x_ref.shape[1]
    y_a = _dot(ya_ref[...], wa_ref[...])
    y_b = _dot(yb_ref[...], wb_ref[...])
    gates = jax.nn.sigmoid(g_ref[...].astype(F32) + bg_ref[...])
    merged = gates[:, :d] * y_a + gates[:, d:] * y_b
    upd = _dot(merged.astype(BF16), wo_ref[...])
    xn = x_ref[...] + mod_ref[0, 2:3, :] * upd
    inv = lax.rsqrt(jnp.mean(xn * xn, axis=-1, keepdims=True) + EPS)
    o_ref[...] = (xn * inv) * fg_ref[...]


def _merge(ya, yb, proj, gcol, x2, mod3, b_gate, wa, wb, wo, final_g, seq, tm):
    m, d = x2.shape
    blocks_per_seq = seq // tm
    return pl.pallas_call(
        _merge_kernel,
        out_shape=jax.ShapeDtypeStruct((m, d), F32),
        grid=(m // tm,),
        in_specs=[
            pl.BlockSpec((tm, D_BRANCH), lambda i: (i, 0)),
            pl.BlockSpec((tm, D_BRANCH), lambda i: (i, 0)),
            pl.BlockSpec((tm, N_BRANCH * d), lambda i: (i, gcol)),
            pl.BlockSpec((tm, d), lambda i: (i, 0)),
            pl.BlockSpec((1, 3, d), lambda i: (i // blocks_per_seq, 0, 0)),
            pl.BlockSpec((1, N_BRANCH * d), lambda i: (0, 0)),
            pl.BlockSpec((D_BRANCH, d), lambda i: (0, 0)),
            pl.BlockSpec((D_BRANCH, d), lambda i: (0, 0)),
            pl.BlockSpec((d, d), lambda i: (0, 0)),
            pl.BlockSpec((1, d), lambda i: (0, 0)),
        ],
        out_specs=pl.BlockSpec((tm, d), lambda i: (i, 0)),
        compiler_params=pltpu.CompilerParams(
            dimension_semantics=("arbitrary",), vmem_limit_bytes=VMEM_LIMIT),
        name="merge",
    )(ya, yb, proj, x2, mod3, b_gate, wa, wb, wo, final_g)


def kernel(x, c, w_ada, b_ada, norm_g, w_in, b_forget, w_o_sb, w_o_fox, b_gate, w_out, final_g):
    b, s, d = x.shape
    n_qkvz = 8 * D_BRANCH
    assert w_ada.shape[0] == 1, "single-layer trunk only"
    assert w_in.shape[2] == n_qkvz + N_HEADS + N_BRANCH * d
    assert s % Q_BLOCK == 0 and s % SB_Q_BLOCK == 0 and d % LANES == 0
    qk_scale = HEAD_DIM ** -0.5 * LOG2E

    w = w_in[0]
    col_scale = jnp.ones((n_qkvz,), F32)
    col_scale = col_scale.at[0:D_BRANCH].set(qk_scale)
    col_scale = col_scale.at[4 * D_BRANCH:5 * D_BRANCH].set(qk_scale)
    w_main = _wprep(w_in, col_scale, n_qkvz, N_HEADS, n_qkvz + N_BRANCH * d, tn=1024)
    w_f = jnp.repeat(w[:, n_qkvz:n_qkvz + N_HEADS], FEAT_LANES, axis=1).astype(BF16)
    bf_spread = jnp.repeat(b_forget[0], FEAT_LANES).reshape(1, LANES)

    x2 = x.reshape(b * s, d)
    mod3 = jnp.transpose(_mod(c, w_ada[0], b_ada[0]), (1, 0, 2))
    proj, f = _proj(x2, mod3, norm_g[0].reshape(1, d), w_main, w_f, s, tm=512, tn=1024)
    proj3 = proj.reshape(b, s, proj.shape[1])
    qfeat, kfeat = _fprep(f.reshape(b, s, LANES), bf_spread)
    nb = D_BRANCH // LANES
    ya = _sb_attn(proj3, 0)
    yb = _fox_attn(proj3, qfeat, kfeat, 4 * nb)
    out = _merge(ya.reshape(b * s, D_BRANCH), yb.reshape(b * s, D_BRANCH), proj,
                 n_qkvz // (N_BRANCH * d), x2, mod3, b_gate[0].reshape(1, N_BRANCH * d),
                 w_o_sb[0].astype(BF16), w_o_fox[0].astype(BF16), w_out[0].astype(BF16),
                 final_g.reshape(1, d), s, tm=512)
    return out.reshape(b, s, d)
```

```python
import functools

import jax
import jax.numpy as jnp
from jax import lax
from jax.experimental import pallas as pl
from jax.experimental.pallas import tpu as pltpu

F32 = jnp.float32
BF16 = jnp.bfloat16

HEAD_DIM = 64
N_HEADS = 8
D_BRANCH = N_HEADS * HEAD_DIM
N_BRANCH = 2
EPS = 1e-6
NEG_BIG = -1e30
LOG2E = 1.4426950408889634
EXP2_CLAMP = 126.0
UNDERFLOW_CUT = 152.0

LANES = 128
HEADS_PER_STEP = LANES // HEAD_DIM
FEAT_LANES = LANES // N_HEADS
K_BLOCK = 256
Q_BLOCK = 8 * K_BLOCK
SB_Q_BLOCK = 8 * K_BLOCK
NEAR_BLOCKS = 3

VMEM_LIMIT = 48 * 1024 * 1024


def _dot(a, b):
    return jnp.dot(a, b, preferred_element_type=F32)


def _dot_nt(a, b):
    return lax.dot_general(a, b, (((1,), (1,)), ((), ())), preferred_element_type=F32)


def _split2(x):
    hi = x.astype(BF16)
    lo = (x - hi.astype(F32)).astype(BF16)
    return hi, lo


def _split3(x):
    hi = x.astype(BF16)
    r1 = x - hi.astype(F32)
    mid = r1.astype(BF16)
    lo = (r1 - mid.astype(F32)).astype(BF16)
    return hi, mid, lo


def _log_sigmoid(x):
    return jnp.minimum(x, 0.0) - jnp.log(1.0 + jnp.exp(-jnp.abs(x)))


def _mod_kernel(c_ref, w_ref, b_ref, o_ref):
    c_hi, c_lo = _split2(c_ref[...])
    w_hi, w_lo = _split2(w_ref[...])
    acc = _dot(c_hi, w_hi) + _dot(c_hi, w_lo) + _dot(c_lo, w_hi)
    o_ref[0] = acc + b_ref[...]


def _mod(c, w_ada, b_ada):
    b, d = c.shape
    return pl.pallas_call(
        _mod_kernel,
        out_shape=jax.ShapeDtypeStruct((3, b, d), F32),
        grid=(3,),
        in_specs=[
            pl.BlockSpec((b, d), lambda j: (0, 0)),
            pl.BlockSpec((d, d), lambda j: (0, j)),
            pl.BlockSpec((1, d), lambda j: (0, j)),
        ],
        out_specs=pl.BlockSpec((1, b, d), lambda j: (j, 0, 0)),
        compiler_params=pltpu.CompilerParams(
            dimension_semantics=("arbitrary",), vmem_limit_bytes=VMEM_LIMIT),
        name="mod",
    )(c, w_ada, b_ada.reshape(1, 3 * d))


def _proj_kernel(x_ref, mod_ref, g_ref, w_ref, wg_ref, wf_ref, o_ref, f_ref, *, tn):
    x = x_ref[...]
    inv = lax.rsqrt(jnp.mean(x * x, axis=-1, keepdims=True) + EPS)
    shift = mod_ref[0, 0:1, :]
    gain = g_ref[...] * (1.0 + mod_ref[0, 1:2, :])
    hb = ((x * inv) * gain + shift).astype(BF16)
    f_ref[...] = _dot(hb, wf_ref[...])
    n_qkvz = w_ref.shape[1]
    for c in range(0, n_qkvz, tn):
        o_ref[:, c:c + tn] = _dot(hb, w_ref[:, c:c + tn]).astype(BF16)
    for c in range(0, wg_ref.shape[1], tn):
        o_ref[:, n_qkvz + c:n_qkvz + c + tn] = _dot(hb, wg_ref[:, c:c + tn]).astype(BF16)


def _proj(x2, mod3, norm_g, w_qkvz, w_gate, w_f, seq, tm, tn):
    m, d = x2.shape
    n = w_qkvz.shape[1] + w_gate.shape[1]
    assert w_qkvz.shape[1] % tn == 0 and w_gate.shape[1] % tn == 0
    blocks_per_seq = seq // tm
    resident = dict(pipeline_mode=pl.Buffered(1))
    return pl.pallas_call(
        functools.partial(_proj_kernel, tn=tn),
        out_shape=(jax.ShapeDtypeStruct((m, n), BF16),
                   jax.ShapeDtypeStruct((m, LANES), F32)),
        grid=(m // tm,),
        in_specs=[
            pl.BlockSpec((tm, d), lambda i: (i, 0)),
            pl.BlockSpec((1, 3, d), lambda i: (i // blocks_per_seq, 0, 0)),
            pl.BlockSpec((1, d), lambda i: (0, 0)),
            pl.BlockSpec(w_qkvz.shape, lambda i: (0, 0), **resident),
            pl.BlockSpec(w_gate.shape, lambda i: (0, 0), **resident),
            pl.BlockSpec((d, LANES), lambda i: (0, 0), **resident),
        ],
        out_specs=(pl.BlockSpec((tm, n), lambda i: (i, 0)),
                   pl.BlockSpec((tm, LANES), lambda i: (i, 0))),
        compiler_params=pltpu.CompilerParams(
            dimension_semantics=("arbitrary",), vmem_limit_bytes=VMEM_LIMIT),
        name="proj",
    )(x2, mod3, norm_g, w_qkvz, w_gate, w_f)


def _fprep_kernel(f_ref, bf_ref, qf_ref, kf_ref, *, seq):
    blk = K_BLOCK
    row = lax.broadcasted_iota(jnp.int32, (blk, blk), 0)
    col = lax.broadcasted_iota(jnp.int32, (blk, blk), 1)
    tri = jnp.where(col <= row, 1.0, 0.0).astype(BF16)
    r = lax.broadcasted_iota(jnp.int32, (blk, LANES), 1) % FEAT_LANES
    part = r % 3
    q_has_f, k_has_f = r < 3, (r >= 3) & (r < 6)
    q_const = jnp.where(k_has_f, 1.0, 0.0)
    k_const = jnp.where(q_has_f, 1.0, 0.0)
    local = []
    for i in range(seq // blk):
        lf = _log_sigmoid(f_ref[0, pl.ds(i * blk, blk), :] + bf_ref[...])
        hi, mid, lo = _split3(lf)
        local.append((_dot(tri, hi) + _dot(tri, mid)) + _dot(tri, lo))
    offset = jnp.zeros((1, LANES), F32)
    for i in range(seq // blk):
        rows = pl.ds(i * blk, blk)
        cum = local[i] + offset
        offset = offset + local[i][blk - 1:blk, :]
        c_hi, c_mid, c_lo = (p.astype(F32) for p in _split3(cum * LOG2E))
        f_part = jnp.where(part == 0, c_hi, jnp.where(part == 1, c_mid, c_lo))
        qf_ref[0, rows, :] = jnp.where(q_has_f, f_part, q_const).astype(BF16)
        kf_ref[0, rows, :] = jnp.where(k_has_f, -f_part, k_const).astype(BF16)


def _fprep(f3, bf_spread):
    b, s, _ = f3.shape
    return pl.pallas_call(
        functools.partial(_fprep_kernel, seq=s),
        out_shape=(jax.ShapeDtypeStruct((b, s, LANES), BF16),
                   jax.ShapeDtypeStruct((b, s, LANES), BF16)),
        grid=(b,),
        in_specs=[pl.BlockSpec((1, s, LANES), lambda i: (i, 0, 0)),
                  pl.BlockSpec((1, LANES), lambda i: (0, 0))],
        out_specs=(pl.BlockSpec((1, s, LANES), lambda i: (i, 0, 0)),
                   pl.BlockSpec((1, s, LANES), lambda i: (i, 0, 0))),
        compiler_params=pltpu.CompilerParams(
            dimension_semantics=("arbitrary",), vmem_limit_bytes=VMEM_LIMIT),
        name="fprep",
    )(f3, bf_spread)


def _silu(z):
    return z * jax.nn.sigmoid(z)


def _head_mask(shape, head_in_step):
    lane = lax.broadcasted_iota(jnp.int32, shape, len(shape) - 1)
    return (lane // HEAD_DIM) == head_in_step


def _sb_kernel(q_ref, k_ref, v_ref, z_ref, o_ref, kx_ref, t_ref):
    tk = K_BLOCK
    tq = q_ref.shape[1]
    nsub = tq // tk
    qi = pl.program_id(2)

    @pl.when(qi == 0)
    def _init():
        k = k_ref[0].astype(F32)
        for h in range(HEADS_PER_STEP):
            kx_ref[h] = jnp.where(_head_mask(k.shape, h), k, 0.0).T.astype(BF16)
        j = lax.broadcasted_iota(jnp.int32, (tk, tk), 0)
        c = lax.broadcasted_iota(jnp.int32, (tk, tk), 1)
        t_ref[...] = jnp.where(j > c, 1.0, 0.0).astype(BF16)

    q = q_ref[0]
    row = lax.broadcasted_iota(jnp.int32, (tk, tk), 0)
    col = lax.broadcasted_iota(jnp.int32, (tk, tk), 1)
    strictly_lower = col < row

    def mask_top(x):
        top = jnp.where(strictly_lower, x[:tk], 0.0)
        return top if x.shape[0] == tk else jnp.concatenate([top, x[tk:]], axis=0)

    def tile(q_rows, h, key_start, carry, acc, on_diagonal):
        keys = pl.ds(key_start, tk)
        z = _dot(q_rows, kx_ref[h, :, keys])
        lg = jnp.log(1.0 + jnp.exp2(jnp.minimum(z, EXP2_CLAMP))) * LOG2E
        sp = jnp.maximum(lg, z)
        ls = z - sp
        if on_diagonal:
            sp = mask_top(sp)
        sp_b = sp.astype(BF16)
        after = _dot(sp_b, t_ref[...]) + carry
        w = jnp.exp2(ls - after)
        if on_diagonal:
            w = mask_top(w)
        acc = acc + _dot(w.astype(BF16), v_ref[0, keys, :])
        return after[:, 0:1] + sp_b[:, 0:1].astype(F32), acc

    heads = range(HEADS_PER_STEP)

    def run(tiles, carry, acc):
        carry, acc = [list(c) for c in carry], [list(a) for a in acc]
        for r0, nr, kb, on_diagonal in tiles:
            for h in heads:
                c_out, a_out = tile(q[r0 * tk:(r0 + nr) * tk], h, kb * tk,
                                    jnp.concatenate(carry[h][r0:r0 + nr], axis=0),
                                    jnp.concatenate(acc[h][r0:r0 + nr], axis=0), on_diagonal)
                for i in range(nr):
                    rows = slice(i * tk, (i + 1) * tk)
                    carry[h][r0 + i], acc[h][r0 + i] = c_out[rows], a_out[rows]
        return carry, acc

    def sweep(qb):
        first = qb * nsub
        carry = [[jnp.zeros((tk, 1), F32)] * nsub for _ in heads]
        acc = [[jnp.zeros((tk, LANES), F32)] * nsub for _ in heads]
        n_blocks = first + nsub

        def band(lo, hi):
            tiles = []
            for kb in range(n_blocks - 1 - lo, -1, -1):
                r0, r1 = max(kb + lo - first, 0), min(kb + hi - first, nsub)
                if r0 < r1:
                    tiles.append((r0, r1 - r0, kb, lo == 0 and kb >= first))
            return tiles

        carry, acc = run(band(0, NEAR_BLOCKS), carry, acc)

        def rest(done, stages, carry, acc):
            rows_left = range(max(done - first, 0), nsub)
            if not stages or not rows_left:
                return acc
            least = functools.reduce(jnp.minimum, [carry[h][r] for h in heads for r in rows_left])

            def more():
                c2, a2 = run(band(done, done + stages[0]), carry, acc)
                return rest(done + stages[0], stages[1:], c2, a2)

            return lax.cond(jnp.min(least) < UNDERFLOW_CUT, more, lambda: acc)

        acc = rest(NEAR_BLOCKS, (1, n_blocks), carry, acc)
        outs = [jnp.concatenate(acc[h], axis=0) for h in heads]
        y = jnp.where(_head_mask(outs[0].shape, 0), outs[0], outs[1])
        o_ref[0] = (y * _silu(z_ref[0].astype(F32))).astype(BF16)

    for qb in range(k_ref.shape[1] // tq):
        pl.when(qi == qb)(functools.partial(sweep, qb))


def _sb_attn(proj3, col0):
    b, s, _ = proj3.shape
    tq = SB_Q_BLOCK
    nb = D_BRANCH // LANES
    return pl.pallas_call(
        _sb_kernel,
        out_shape=jax.ShapeDtypeStruct((b, s, D_BRANCH), BF16),
        grid=(b, nb, s // tq),
        in_specs=[
            pl.BlockSpec((1, tq, LANES), lambda bi, hp, qi: (bi, qi, col0 + hp)),
            pl.BlockSpec((1, s, LANES), lambda bi, hp, qi: (bi, 0, col0 + nb + hp)),
            pl.BlockSpec((1, s, LANES), lambda bi, hp, qi: (bi, 0, col0 + 2 * nb + hp)),
            pl.BlockSpec((1, tq, LANES), lambda bi, hp, qi: (bi, qi, col0 + 3 * nb + hp)),
        ],
        out_specs=pl.BlockSpec((1, tq, LANES), lambda bi, hp, qi: (bi, qi, hp)),
        scratch_shapes=[pltpu.VMEM((HEADS_PER_STEP, LANES, s), BF16),
                        pltpu.VMEM((K_BLOCK, K_BLOCK), BF16)],
        compiler_params=pltpu.CompilerParams(
            dimension_semantics=("arbitrary", "arbitrary", "arbitrary"),
            vmem_limit_bytes=VMEM_LIMIT),
        name="sb_attn",
    )(proj3, proj3, proj3, proj3)


def _fox_kernel(q_ref, k_ref, v_ref, z_ref, qf_ref, kf_ref, o_ref, kx_ref, vx_ref, zs_ref):
    tq = q_ref.shape[1]
    hp = pl.program_id(1)
    qi = pl.program_id(2)

    @pl.when(qi == 0)
    def _init():
        k = k_ref[0].astype(F32)
        kf = kf_ref[0].astype(F32)
        feat_lane = lax.broadcasted_iota(jnp.int32, kf.shape, 1) // FEAT_LANES
        for h in range(HEADS_PER_STEP):
            km = jnp.where(_head_mask(k.shape, h), k, 0.0)
            kfm = jnp.where(feat_lane == hp * HEADS_PER_STEP + h, kf, 0.0)
            kx_ref[h] = jnp.concatenate([km.T, kfm.T], axis=0).astype(BF16)
        v = v_ref[0].astype(F32)
        for h in range(HEADS_PER_STEP):
            vx_ref[h] = jnp.where(_head_mask(v.shape, h), v, 1.0).astype(BF16)

    for qb in range(k_ref.shape[1] // tq):
        pl.when(qi == qb)(functools.partial(
            _fox_sweep, qb, q_ref, z_ref, qf_ref, o_ref, kx_ref, vx_ref, zs_ref))


def _fox_sweep(qb, q_ref, z_ref, qf_ref, o_ref, kx_ref, vx_ref, zs_ref):
    tk = K_BLOCK
    tq = q_ref.shape[1]
    nsub = tq // tk
    row = lax.broadcasted_iota(jnp.int32, (tk, tk), 0)
    col = lax.broadcasted_iota(jnp.int32, (tk, tk), 1)
    causal = col <= row

    tiles = [(c, nsub - c, qb * nsub + c, True) for c in range(nsub)]
    tiles += [(0, nsub, kb, False) for kb in range(qb * nsub - 1, -1, -1)]

    qx = jnp.concatenate([q_ref[0], qf_ref[0]], axis=1)
    heads = range(HEADS_PER_STEP)
    zmax = [[None] * nsub for _ in heads]
    for r0, nr, kb, on_diagonal in tiles:
        rows, keys = pl.ds(r0 * tk, nr * tk), pl.ds(kb * tk, tk)
        for h in heads:
            z = _dot(qx[r0 * tk:(r0 + nr) * tk], kx_ref[h, :, keys])
            if on_diagonal:
                top = jnp.where(causal, z[:tk], NEG_BIG)
                z = top if nr == 1 else jnp.concatenate([top, z[tk:]], axis=0)
            zs_ref[h, rows, keys] = z
            zm = jnp.maximum(z[:, :LANES], z[:, LANES:])
            for i in range(nr):
                part, old = zm[i * tk:(i + 1) * tk], zmax[h][r0 + i]
                zmax[h][r0 + i] = part if old is None else jnp.maximum(old, part)
    m = [[jnp.broadcast_to(jnp.max(zm, axis=1, keepdims=True), (tk, LANES)) for zm in zmax[h]]
         for h in heads]

    acc = [[jnp.zeros((tk, LANES), F32)] * nsub for _ in heads]
    for r0, nr, kb, _ in tiles:
        rows, keys = pl.ds(r0 * tk, nr * tk), pl.ds(kb * tk, tk)
        for h in heads:
            z = zs_ref[h, rows, keys]
            m_rows = m[h][r0] if nr == 1 else jnp.concatenate(m[h][r0:r0 + nr], axis=0)
            p = jnp.exp2(jnp.concatenate([z[:, :LANES] - m_rows, z[:, LANES:] - m_rows], axis=1))
            pv = _dot(p.astype(BF16), vx_ref[h, keys, :])
            for i in range(nr):
                acc[h][r0 + i] = acc[h][r0 + i] + pv[i * tk:(i + 1) * tk]
    outs = []
    for h in heads:
        a = jnp.concatenate(acc[h], axis=0)
        outs.append(a / pltpu.roll(a, HEAD_DIM, axis=1))
    y = jnp.where(_head_mask(outs[0].shape, 0), outs[0], outs[1])
    o_ref[0] = (y * _silu(z_ref[0].astype(F32))).astype(BF16)


def _fox_attn(proj3, qfeat, kfeat, col0):
    b, s, _ = proj3.shape
    tq = Q_BLOCK
    nb = D_BRANCH // LANES
    return pl.pallas_call(
        _fox_kernel,
        out_shape=jax.ShapeDtypeStruct((b, s, D_BRANCH), BF16),
        grid=(b, nb, s // tq),
        in_specs=[
            pl.BlockSpec((1, tq, LANES), lambda bi, hp, qi: (bi, qi, col0 + hp)),
            pl.BlockSpec((1, s, LANES), lambda bi, hp, qi: (bi, 0, col0 + nb + hp)),
            pl.BlockSpec((1, s, LANES), lambda bi, hp, qi: (bi, 0, col0 + 2 * nb + hp)),
            pl.BlockSpec((1, tq, LANES), lambda bi, hp, qi: (bi, qi, col0 + 3 * nb + hp)),
            pl.BlockSpec((1, tq, LANES), lambda bi, hp, qi: (bi, qi, 0)),
            pl.BlockSpec((1, s, LANES), lambda bi, hp, qi: (bi, 0, 0)),
        ],
        out_specs=pl.BlockSpec((1, tq, LANES), lambda bi, hp, qi: (bi, qi, hp)),
        scratch_shapes=[pltpu.VMEM((HEADS_PER_STEP, 2 * LANES, s), BF16),
                        pltpu.VMEM((HEADS_PER_STEP, s, LANES), BF16),
                        pltpu.VMEM((HEADS_PER_STEP, tq, s), F32)],
        compiler_params=pltpu.CompilerParams(
            dimension_semantics=("arbitrary", "arbitrary", "arbitrary"),
            vmem_limit_bytes=VMEM_LIMIT),
        name="fox_attn",
    )(proj3, proj3, proj3, proj3, qfeat, kfeat)


def _merge_kernel(ya_ref, yb_ref, g_ref, x_ref, mod_ref, bg_ref, wa_ref, wb_ref, wo_ref,
                  fg_ref, o_ref):
    d = x_ref.shape[1]
    y_a = _dot(ya_ref[...], wa_ref[...])
    y_b = _dot(yb_ref[...], wb_ref[...])
    gates = jax.nn.sigmoid(g_ref[...].astype(F32) + bg_ref[...])
    merged = gates[:, :d] * y_a + gates[:, d:] * y_b
    upd = _dot(merged.astype(BF16), wo_ref[...])
    xn = x_ref[...] + mod_ref[0, 2:3, :] * upd
    inv = lax.rsqrt(jnp.mean(xn * xn, axis=-1, keepdims=True) + EPS)
    o_ref[...] = (xn * inv) * fg_ref[...]


def _merge(ya, yb, proj, gcol, x2, mod3, b_gate, wa, wb, wo, final_g, seq, tm):
    m, d = x2.shape
    blocks_per_seq = seq // tm
    return pl.pallas_call(
        _merge_kernel,
        out_shape=jax.ShapeDtypeStruct((m, d), F32),
        grid=(m // tm,),
        in_specs=[
            pl.BlockSpec((tm, D_BRANCH), lambda i: (i, 0)),
            pl.BlockSpec((tm, D_BRANCH), lambda i: (i, 0)),
            pl.BlockSpec((tm, N_BRANCH * d), lambda i: (i, gcol)),
            pl.BlockSpec((tm, d), lambda i: (i, 0)),
            pl.BlockSpec((1, 3, d), lambda i: (i // blocks_per_seq, 0, 0)),
            pl.BlockSpec((1, N_BRANCH * d), lambda i: (0, 0)),
            pl.BlockSpec((D_BRANCH, d), lambda i: (0, 0)),
            pl.BlockSpec((D_BRANCH, d), lambda i: (0, 0)),
            pl.BlockSpec((d, d), lambda i: (0, 0)),
            pl.BlockSpec((1, d), lambda i: (0, 0)),
        ],
        out_specs=pl.BlockSpec((tm, d), lambda i: (i, 0)),
        compiler_params=pltpu.CompilerParams(
            dimension_semantics=("arbitrary",), vmem_limit_bytes=VMEM_LIMIT),
        name="merge",
    )(ya, yb, proj, x2, mod3, b_gate, wa, wb, wo, final_g)


def kernel(x, c, w_ada, b_ada, norm_g, w_in, b_forget, w_o_sb, w_o_fox, b_gate, w_out, final_g):
    b, s, d = x.shape
    n_qkvz = 8 * D_BRANCH
    assert w_ada.shape[0] == 1, "single-layer trunk only"
    assert w_in.shape[2] == n_qkvz + N_HEADS + N_BRANCH * d
    assert s % Q_BLOCK == 0 and s % SB_Q_BLOCK == 0 and d % LANES == 0
    qk_scale = HEAD_DIM ** -0.5 * LOG2E

    w = w_in[0]
    col_scale = jnp.ones((n_qkvz,), F32)
    col_scale = col_scale.at[0:D_BRANCH].set(qk_scale)
    col_scale = col_scale.at[4 * D_BRANCH:5 * D_BRANCH].set(qk_scale)
    w_qkvz = (w[:, :n_qkvz] * col_scale).astype(BF16)
    w_gate = w[:, n_qkvz + N_HEADS:].astype(BF16)
    w_f = jnp.repeat(w[:, n_qkvz:n_qkvz + N_HEADS], FEAT_LANES, axis=1).astype(BF16)
    bf_spread = jnp.repeat(b_forget[0], FEAT_LANES).reshape(1, LANES)

    x2 = x.reshape(b * s, d)
    mod3 = jnp.transpose(_mod(c, w_ada[0], b_ada[0]), (1, 0, 2))
    proj, f = _proj(x2, mod3, norm_g[0].reshape(1, d), w_qkvz, w_gate, w_f, s,
                    tm=512, tn=1024)
    proj3 = proj.reshape(b, s, proj.shape[1])
    qfeat, kfeat = _fprep(f.reshape(b, s, LANES), bf_spread)
    nb = D_BRANCH // LANES
    ya = _sb_attn(proj3, 0)
    yb = _fox_attn(proj3, qfeat, kfeat, 4 * nb)
    out = _merge(ya.reshape(b * s, D_BRANCH), yb.reshape(b * s, D_BRANCH), proj,
                 n_qkvz // (N_BRANCH * d), x2, mod3, b_gate[0].reshape(1, N_BRANCH * d),
                 w_o_sb[0].astype(BF16), w_o_fox[0].astype(BF16), w_out[0].astype(BF16),
                 final_g.reshape(1, d), s, tm=512)
    return out.reshape(b, s, d)
```
